```python
import jax, jax.numpy as jnp
from jax import lax
import numpy as np

D_MODEL = 4096
BATCH = 1
SEQ = 8192
DEPTH = 4

N_ATTN_HEADS = 16
HEAD_DIM = 128
ATTN_WIDTH = N_ATTN_HEADS * HEAD_DIM
CONV_CHANNELS = D_MODEL - ATTN_WIDTH
MIX_WIDTH = ATTN_WIDTH + CONV_CHANNELS
IN_WIDTH = 3 * ATTN_WIDTH + 2 * CONV_CHANNELS
CONV_WIDTH = 31
D_FF = (3 * D_MODEL) // 2
ROPE_THETA = 500000.0
ROT_DIM = HEAD_DIM // 4
DILATED_BRANCHES = ((128, 1), (512, 4), (2048, 16))
BLOCK = 128
RMS_EPS = 1e-5
LN_EPS = 1e-5
FFN_RESIDUAL_SCALE = 0.5
MASK_VALUE = -1e30

kernel_name = "hymba_longnet_conformer_macaron"


def rms_norm(x, g):
    xf = x.astype(jnp.float32)
    y = xf * lax.rsqrt(jnp.mean(xf * xf, axis=-1, keepdims=True) + RMS_EPS)
    return (y * g.astype(jnp.float32)).astype(x.dtype)


def layer_norm(x, g, b):
    xf = x.astype(jnp.float32)
    mu = jnp.mean(xf, axis=-1, keepdims=True)
    xc = xf - mu
    y = xc * lax.rsqrt(jnp.mean(xc * xc, axis=-1, keepdims=True) + LN_EPS)
    return (y * g.astype(jnp.float32) + b.astype(jnp.float32)).astype(x.dtype)


def swiglu_ffn(x, w_gate, w_up, w_down):
    return (jax.nn.silu(x @ w_gate) * (x @ w_up)) @ w_down


def rotary_tables(seq):
    pos = jnp.arange(seq, dtype=jnp.float32)
    inv_freq = ROPE_THETA ** (-(jnp.arange(0, ROT_DIM, 2, dtype=jnp.float32) / ROT_DIM))
    ang = pos[:, None] * inv_freq[None, :]
    return jnp.cos(ang), jnp.sin(ang)


def partial_rotary(x, cos, sin):
    half = ROT_DIM // 2
    c = cos[None, :, None, :].astype(x.dtype)
    s = sin[None, :, None, :].astype(x.dtype)
    x1, x2, rest = x[..., :half], x[..., half:ROT_DIM], x[..., ROT_DIM:]
    return jnp.concatenate([x1 * c - x2 * s, x2 * c + x1 * s, rest], axis=-1)


def dilated_branch(q, k, v, window, dilation):
    B, S, H, Dh = q.shape
    w_sub = window // dilation
    unit = dilation * BLOCK
    s_pad = -(-S // unit) * unit
    nb = s_pad // unit

    def to_blocks(a):
        a = jnp.pad(a, ((0, 0), (0, s_pad - S), (0, 0), (0, 0)))
        return a.reshape(B, nb, BLOCK, dilation, H, Dh)

    def with_prev(a):
        prev = jnp.pad(a[:, :-1], ((0, 0), (1, 0), (0, 0), (0, 0), (0, 0), (0, 0)))
        return jnp.concatenate([prev, a], axis=2)

    qb = to_blocks(q).astype(jnp.float32)
    kk = with_prev(to_blocks(k)).astype(jnp.float32)
    vv = with_prev(to_blocks(v)).astype(jnp.float32)

    scores = jnp.einsum('bnqrhd,bnkrhd->bnrhqk', qb, kk) * (Dh ** -0.5)
    qi = jnp.arange(BLOCK)[:, None] + BLOCK
    kj = jnp.arange(2 * BLOCK)[None, :]
    delta = qi - kj
    band = (delta >= 0) & (delta <= w_sub)
    valid = (jnp.arange(nb)[:, None, None] > 0) | (kj >= BLOCK)[None]
    mask = (band[None] & valid)[None, :, None, None]
    scores = jnp.where(mask, scores, MASK_VALUE)
    m = jnp.max(scores, axis=-1, keepdims=True)
    p = jnp.exp(scores - m)
    den = jnp.sum(p, axis=-1)
    out = jnp.einsum('bnrhqk,bnkrhd->bnqrhd', p, vv)
    den_t = den.transpose(0, 1, 4, 2, 3)
    out = out / den_t[..., None]
    lse = (m[..., 0] + jnp.log(den)).transpose(0, 1, 4, 2, 3)
    out = out.reshape(B, s_pad, H, Dh)[:, :S]
    lse = lse.reshape(B, s_pad, H)[:, :S]
    return out, lse


def longnet_attention(q, k, v):
    outs, lses = [], []
    for window, dilation in DILATED_BRANCHES:
        o, l = dilated_branch(q, k, v, window, dilation)
        outs.append(o)
        lses.append(l)
    w = jax.nn.softmax(jnp.stack(lses, axis=0), axis=0)
    o = jnp.einsum('nbsh,nbshd->bshd', w, jnp.stack(outs, axis=0))
    return o.astype(q.dtype)


def conformer_conv(a, gate, w_dw, b_dw, ln_g, ln_b):
    h = a * jax.nn.sigmoid(gate)
    h = lax.conv_general_dilated(
        h, w_dw[:, None, :].astype(h.dtype), window_strides=(1,),
        padding=((CONV_WIDTH - 1, 0),), dimension_numbers=('NWC', 'WIO', 'NWC'),
        feature_group_count=CONV_CHANNELS) + b_dw.astype(h.dtype)
    h = layer_norm(h, ln_g, ln_b)
    return jax.nn.silu(h)


def setup_inputs(seed: int = 0) -> dict:
    key = jax.random.key(seed)
    ks = jax.random.split(key, 20)
    f32 = jnp.float32

    def normal(k, shape, scale):
        return jax.random.normal(k, shape, dtype=f32) * scale

    def gain(k, shape):
        return 1.0 + 0.01 * jax.random.normal(k, shape, dtype=f32)

    return {
        "x": normal(ks[0], (BATCH, SEQ, D_MODEL), 1.0),
        "ffn1_norm": gain(ks[1], (DEPTH, D_MODEL)),
        "ffn1_w_gate": normal(ks[2], (DEPTH, D_MODEL, D_FF), D_MODEL ** -0.5),
        "ffn1_w_up": normal(ks[3], (DEPTH, D_MODEL, D_FF), D_MODEL ** -0.5),
        "ffn1_w_down": normal(ks[4], (DEPTH, D_FF, D_MODEL), D_FF ** -0.5),
        "mix_norm": gain(ks[5], (DEPTH, D_MODEL)),
        "w_in": normal(ks[6], (DEPTH, D_MODEL, IN_WIDTH), D_MODEL ** -0.5),
        "conv_w": normal(ks[7], (DEPTH, CONV_WIDTH, CONV_CHANNELS), CONV_WIDTH ** -0.5),
        "conv_b": normal(ks[8], (DEPTH, CONV_CHANNELS), 0.01),
        "conv_ln_g": gain(ks[9], (DEPTH, CONV_CHANNELS)),
        "conv_ln_b": normal(ks[10], (DEPTH, CONV_CHANNELS), 0.01),
        "attn_out_norm": gain(ks[11], (DEPTH, ATTN_WIDTH)),
        "conv_out_norm": gain(ks[12], (DEPTH, CONV_CHANNELS)),
        "w_out": normal(ks[13], (DEPTH, MIX_WIDTH, D_MODEL), MIX_WIDTH ** -0.5),
        "ffn2_norm": gain(ks[14], (DEPTH, D_MODEL)),
        "ffn2_w_gate": normal(ks[15], (DEPTH, D_MODEL, D_FF), D_MODEL ** -0.5),
        "ffn2_w_up": normal(ks[16], (DEPTH, D_MODEL, D_FF), D_MODEL ** -0.5),
        "ffn2_w_down": normal(ks[17], (DEPTH, D_FF, D_MODEL), D_FF ** -0.5),
        "final_norm": gain(ks[18], (D_MODEL,)),
    }


def reference(x, ffn1_norm, ffn1_w_gate, ffn1_w_up, ffn1_w_down, mix_norm, w_in,
              conv_w, conv_b, conv_ln_g, conv_ln_b, attn_out_norm, conv_out_norm, w_out,
              ffn2_norm, ffn2_w_gate, ffn2_w_up, ffn2_w_down, final_norm):
    B, S, _ = x.shape
    cos, sin = rotary_tables(S)
    A, C = ATTN_WIDTH, CONV_CHANNELS
    for l in range(DEPTH):
        x = x + FFN_RESIDUAL_SCALE * swiglu_ffn(rms_norm(x, ffn1_norm[l]),
                                                ffn1_w_gate[l], ffn1_w_up[l], ffn1_w_down[l])
        u = rms_norm(x, mix_norm[l])
        proj = u @ w_in[l]
        q = proj[..., :A].reshape(B, S, N_ATTN_HEADS, HEAD_DIM)
        k = proj[..., A:2 * A].reshape(B, S, N_ATTN_HEADS, HEAD_DIM)
        v = proj[..., 2 * A:3 * A].reshape(B, S, N_ATTN_HEADS, HEAD_DIM)
        glu_a = proj[..., 3 * A:3 * A + C]
        glu_g = proj[..., 3 * A + C:]
        q = partial_rotary(q, cos, sin)
        k = partial_rotary(k, cos, sin)
        attn = longnet_attention(q, k, v).reshape(B, S, A)
        conv = conformer_conv(glu_a, glu_g, conv_w[l], conv_b[l], conv_ln_g[l], conv_ln_b[l])
        mixed = jnp.concatenate([rms_norm(attn, attn_out_norm[l]),
                                 rms_norm(conv, conv_out_norm[l])], axis=-1)
        x = x + mixed @ w_out[l]
        x = x + FFN_RESIDUAL_SCALE * swiglu_ffn(rms_norm(x, ffn2_norm[l]),
                                                ffn2_w_gate[l], ffn2_w_up[l], ffn2_w_down[l])
    return rms_norm(x, final_norm)
```

```python
import functools

import jax
import jax.numpy as jnp
from jax import lax
from jax.experimental import pallas as pl
from jax.experimental.pallas import tpu as pltpu

D_MODEL = 4096
DEPTH = 4
N_HEADS = 16
HEAD_DIM = 128
ATTN_WIDTH = N_HEADS * HEAD_DIM
CONV_CHANNELS = D_MODEL - ATTN_WIDTH
IN_WIDTH = 3 * ATTN_WIDTH + 2 * CONV_CHANNELS
CONV_WIDTH = 31
D_FF = (3 * D_MODEL) // 2
ROPE_THETA = 500000.0
ROT_DIM = HEAD_DIM // 4
DILATED_BRANCHES = ((128, 1), (512, 4), (2048, 16))
BLOCK = 128
RMS_EPS = 1e-5
LN_EPS = 1e-5
FFN_RESIDUAL_SCALE = 0.5
MASK_VALUE = -1e30

LANES = 128
CONV_HALO = 32
MIB = 1024 * 1024

F32 = jnp.float32
BF16 = jnp.bfloat16


def _params(semantics, vmem_mib):
    return pltpu.CompilerParams(dimension_semantics=semantics,
                                vmem_limit_bytes=vmem_mib * MIB)


def _rmsnorm_kernel(x_ref, g_ref, o_ref):
    x = x_ref[...]
    ms = jnp.mean(x * x, axis=-1, keepdims=True)
    o_ref[...] = (x * lax.rsqrt(ms + RMS_EPS) * g_ref[...]).astype(o_ref.dtype)


def _rmsnorm(x, g, layer, out_dtype, tr=256):
    m, d = x.shape
    return pl.pallas_call(
        _rmsnorm_kernel,
        grid=(m // tr,),
        in_specs=[pl.BlockSpec((tr, d), lambda i: (i, 0)),
                  pl.BlockSpec((None, 1, d), lambda i: (layer, 0, 0))],
        out_specs=pl.BlockSpec((tr, d), lambda i: (i, 0)),
        out_shape=jax.ShapeDtypeStruct((m, d), out_dtype),
        compiler_params=_params(("parallel",), 32),
        name="rmsnorm",
    )(x, g)


def _dual_mm_kernel(a_ref, w1_ref, w2_ref, o_ref, *, mode):
    a = a_ref[...]
    p1 = jnp.dot(a, w1_ref[...], preferred_element_type=F32)
    p2 = jnp.dot(a, w2_ref[...], preferred_element_type=F32)
    if mode == "swiglu":
        o = (p1 * jax.nn.sigmoid(p1)) * p2
    else:
        o = p1 * jax.nn.sigmoid(p2)
    o_ref[...] = o.astype(o_ref.dtype)


def _dual_mm(a, w1, w2, layer, off1, off2, n_out, mode, out_dtype, tm=1024, tn=512):
    m, k = a.shape
    o1, o2 = off1 // tn, off2 // tn
    return pl.pallas_call(
        functools.partial(_dual_mm_kernel, mode=mode),
        grid=(m // tm, n_out // tn),
        in_specs=[pl.BlockSpec((tm, k), lambda i, j: (i, 0)),
                  pl.BlockSpec((None, k, tn), lambda i, j: (layer, 0, o1 + j)),
                  pl.BlockSpec((None, k, tn), lambda i, j: (layer, 0, o2 + j))],
        out_specs=pl.BlockSpec((tm, tn), lambda i, j: (i, j)),
        out_shape=jax.ShapeDtypeStruct((m, n_out), out_dtype),
        compiler_params=_params(("parallel", "arbitrary"), 56),
        name="dual_mm_" + mode,
    )(a, w1, w2)


def _res_mm_kernel(*refs, n_a, scale):
    a_refs, w_refs = refs[:n_a], refs[n_a:2 * n_a]
    res_ref, o_ref = refs[2 * n_a], refs[2 * n_a + 1]
    acc = jnp.dot(a_refs[0][...], w_refs[0][...], preferred_element_type=F32)
    for a_ref, w_ref in zip(a_refs[1:], w_refs[1:]):
        acc = acc + jnp.dot(a_ref[...], w_ref[...], preferred_element_type=F32)
    if scale != 1.0:
        acc = scale * acc
    o_ref[...] = res_ref[...] + acc


def _res_mm(a_list, w, layer, res, scale, tm=1024, tn=512):
    n_a = len(a_list)
    m, k = a_list[0].shape
    n = w.shape[-1]
    a_specs = [pl.BlockSpec((tm, k), lambda i, j: (i, 0)) for _ in a_list]
    w_specs = [pl.BlockSpec((None, k, tn), functools.partial(lambda i, j, s: (layer, s, j), s=s))
               for s in range(n_a)]
    return pl.pallas_call(
        functools.partial(_res_mm_kernel, n_a=n_a, scale=scale),
        grid=(m // tm, n // tn),
        in_specs=a_specs + w_specs + [pl.BlockSpec((tm, tn), lambda i, j: (i, j))],
        out_specs=pl.BlockSpec((tm, tn), lambda i, j: (i, j)),
        out_shape=jax.ShapeDtypeStruct((m, n), F32),
        compiler_params=_params(("parallel", "arbitrary"), 56),
        name="res_mm",
    )(*a_list, *([w] * n_a), res)


def _qkv_kernel(a_ref, w_ref, cos_ref, sin_lo_ref, sin_hi_ref, o_ref, *, n_rot_tiles):
    p = jnp.dot(a_ref[...], w_ref[...], preferred_element_type=F32)
    j = pl.program_id(1)
    tn = p.shape[1]

    @pl.when(j < n_rot_tiles)
    def _():
        cos, sin_lo, sin_hi = cos_ref[...], sin_lo_ref[...], sin_hi_ref[...]
        for h in range(tn // HEAD_DIM):
            cols = slice(h * HEAD_DIM, (h + 1) * HEAD_DIM)
            x = p[:, cols]
            r = (x * cos + pltpu.roll(x, HEAD_DIM - ROT_DIM // 2, 1) * sin_lo
                 + pltpu.roll(x, ROT_DIM // 2, 1) * sin_hi)
            o_ref[:, cols] = r.astype(o_ref.dtype)

    @pl.when(j >= n_rot_tiles)
    def _():
        o_ref[...] = p.astype(o_ref.dtype)


def _qkv_mm(a, w, layer, rot, tm=1024, tn=512):
    m, k = a.shape
    n_out = 3 * ATTN_WIDTH
    cos, sin_lo, sin_hi = rot
    tab_spec = pl.BlockSpec((tm, HEAD_DIM), lambda i, j: (i, 0))
    return pl.pallas_call(
        functools.partial(_qkv_kernel, n_rot_tiles=2 * ATTN_WIDTH // tn),
        grid=(m // tm, n_out // tn),
        in_specs=[pl.BlockSpec((tm, k), lambda i, j: (i, 0)),
                  pl.BlockSpec((None, k, tn), lambda i, j: (layer, 0, j)),
                  tab_spec, tab_spec, tab_spec],
        out_specs=pl.BlockSpec((tm, tn), lambda i, j: (i, j)),
        out_shape=jax.ShapeDtypeStruct((m, n_out), BF16),
        compiler_params=_params(("parallel", "arbitrary"), 56),
        name="qkv_mm",
    )(a, w, cos, sin_lo, sin_hi)


def _attn_branch_kernel(q_ref, kp_ref, kc_ref, vp_ref, vc_ref, o_ref, lse_ref, *, w_sub):
    n = pl.program_id(1)
    qi = lax.broadcasted_iota(jnp.int32, (BLOCK, 2 * BLOCK), 0) + BLOCK
    kj = lax.broadcasted_iota(jnp.int32, (BLOCK, 2 * BLOCK), 1)
    delta = qi - kj
    mask = (delta >= 0) & (delta <= w_sub) & ((kj >= BLOCK) | (n > 0))
    scale = HEAD_DIM ** -0.5
    for h in range(N_HEADS):
        cols = slice(h * HEAD_DIM, (h + 1) * HEAD_DIM)
        q = q_ref[:, cols]
        k = jnp.concatenate([kp_ref[:, cols], kc_ref[:, cols]], axis=0)
        v = jnp.concatenate([vp_ref[:, cols], vc_ref[:, cols]], axis=0)
        s = lax.dot_general(q, k, (((1,), (1,)), ((), ())), preferred_element_type=F32) * scale
        s = jnp.where(mask, s, MASK_VALUE)
        m = jnp.max(s, axis=-1, keepdims=True)
        p = jnp.exp(s - m)
        den = jnp.sum(p, axis=-1, keepdims=True)
        o = jnp.dot(p.astype(BF16), v, preferred_element_type=F32)
        o_ref[:, cols] = o * (1.0 / den)
        lse_ref[:, h:h + 1] = m + jnp.log(den)


def _attn_branch(qkv, window, dilation):
    s_len = qkv.shape[0]
    d = dilation
    rows = s_len // d
    nb = rows // BLOCK
    a = ATTN_WIDTH
    view = qkv.reshape(rows, d * 3 * a)

    def spec(part, prev):
        if prev:
            return pl.BlockSpec((BLOCK, a), lambda r, n: (jnp.maximum(n - 1, 0), 3 * r + part))
        return pl.BlockSpec((BLOCK, a), lambda r, n: (n, 3 * r + part))

    out, lse = pl.pallas_call(
        functools.partial(_attn_branch_kernel, w_sub=window // dilation),
        grid=(d, nb),
        in_specs=[spec(0, False), spec(1, True), spec(1, False), spec(2, True), spec(2, False)],
        out_specs=[pl.BlockSpec((BLOCK, a), lambda r, n: (n, r)),
                   pl.BlockSpec((None, BLOCK, N_HEADS), lambda r, n: (r, n, 0))],
        out_shape=[jax.ShapeDtypeStruct((rows, d * a), F32),
                   jax.ShapeDtypeStruct((d, rows, N_HEADS), F32)],
        compiler_params=_params(("parallel", "arbitrary"), 32),
        name="attn_branch_d%d" % d,
    )(view, view, view, view, view)
    out = out.reshape(s_len, a)
    lse = lse.transpose(1, 0, 2).reshape(s_len, N_HEADS)
    return out, lse


def _attn_merge_kernel(o1_ref, o2_ref, o3_ref, l1_ref, l2_ref, l3_ref, g_ref, out_ref, acc_ref):
    l1, l2, l3 = l1_ref[...], l2_ref[...], l3_ref[...]
    mx = jnp.maximum(jnp.maximum(l1, l2), l3)
    e1, e2, e3 = jnp.exp(l1 - mx), jnp.exp(l2 - mx), jnp.exp(l3 - mx)
    inv = 1.0 / (e1 + e2 + e3)
    w1, w2, w3 = e1 * inv, e2 * inv, e3 * inv
    sumsq = jnp.zeros((o1_ref.shape[0], 1), F32)
    for h in range(N_HEADS):
        cols = slice(h * HEAD_DIM, (h + 1) * HEAD_DIM)
        o = (w1[:, h:h + 1] * o1_ref[:, cols] + w2[:, h:h + 1] * o2_ref[:, cols]
             + w3[:, h:h + 1] * o3_ref[:, cols])
        acc_ref[:, cols] = o
        sumsq = sumsq + jnp.sum(o * o, axis=-1, keepdims=True)
    rs = lax.rsqrt(sumsq * (1.0 / ATTN_WIDTH) + RMS_EPS)
    out_ref[...] = (acc_ref[...] * rs * g_ref[...]).astype(out_ref.dtype)


def _attn_merge(outs, lses, g, layer, tr=256):
    s_len, a = outs[0].shape
    o_spec = pl.BlockSpec((tr, a), lambda i: (i, 0))
    l_spec = pl.BlockSpec((tr, N_HEADS), lambda i: (i, 0))
    return pl.pallas_call(
        _attn_merge_kernel,
        grid=(s_len // tr,),
        in_specs=[o_spec, o_spec, o_spec, l_spec, l_spec, l_spec,
                  pl.BlockSpec((None, 1, a), lambda i: (layer, 0, 0))],
        out_specs=pl.BlockSpec((tr, a), lambda i: (i, 0)),
        out_shape=jax.ShapeDtypeStruct((s_len, a), BF16),
        scratch_shapes=[pltpu.VMEM((tr, a), F32)],
        compiler_params=_params(("parallel",), 32),
        name="attn_merge",
    )(*outs, *lses, g)


def _conv_kernel(h_ref, halo_ref, w_ref, b_ref, lng_ref, lnb_ref, g_ref, out_ref,
                 hbuf_ref, cbuf_ref, *, ts, row_chunk, ln_chunk):
    i = pl.program_id(0)
    halo = halo_ref[...]
    hbuf_ref[0:CONV_HALO, :] = jnp.where(i > 0, halo, jnp.zeros_like(halo))
    hbuf_ref[CONV_HALO:, :] = h_ref[...]
    first = CONV_HALO - (CONV_WIDTH - 1)

    def col_group(c, carry):
        cols = pl.ds(pl.multiple_of(c * LANES, LANES), LANES)
        w = w_ref[:, cols]
        b = b_ref[:, cols]
        for rc in range(ts // row_chunk):
            base = rc * row_chunk
            acc = w[0:1, :] * hbuf_ref[pl.ds(base + first, row_chunk), cols]
            for j in range(1, CONV_WIDTH):
                acc = acc + w[j:j + 1, :] * hbuf_ref[pl.ds(base + first + j, row_chunk), cols]
            cbuf_ref[pl.ds(base, row_chunk), cols] = acc + b
        return carry

    lax.fori_loop(0, CONV_CHANNELS // LANES, col_group, 0)

    def norm_rows(rc, carry):
        rows = pl.ds(pl.multiple_of(rc * ln_chunk, ln_chunk), ln_chunk)
        x = cbuf_ref[rows, :]
        mu = jnp.mean(x, axis=-1, keepdims=True)
        xc = x - mu
        var = jnp.mean(xc * xc, axis=-1, keepdims=True)
        y = xc * lax.rsqrt(var + LN_EPS) * lng_ref[...] + lnb_ref[...]
        y = y * jax.nn.sigmoid(y)
        ms = jnp.mean(y * y, axis=-1, keepdims=True)
        out_ref[rows, :] = (y * lax.rsqrt(ms + RMS_EPS) * g_ref[...]).astype(out_ref.dtype)
        return carry

    lax.fori_loop(0, ts // ln_chunk, norm_rows, 0)


def _conv_block(h, conv_w, conv_b, ln_g, ln_b, out_g, layer, ts=256):
    s_len, c = h.shape
    vec_spec = pl.BlockSpec((None, 1, c), lambda i: (layer, 0, 0))
    halo_per_block = ts // CONV_HALO
    return pl.pallas_call(
        functools.partial(_conv_kernel, ts=ts, row_chunk=64, ln_chunk=16),
        grid=(s_len // ts,),
        in_specs=[pl.BlockSpec((ts, c), lambda i: (i, 0)),
                  pl.BlockSpec((CONV_HALO, c), lambda i: (jnp.maximum(i * halo_per_block - 1, 0), 0)),
                  pl.BlockSpec((None, CONV_WIDTH, c), lambda i: (layer, 0, 0)),
                  vec_spec, vec_spec, vec_spec, vec_spec],
        out_specs=pl.BlockSpec((ts, c), lambda i: (i, 0)),
        out_shape=jax.ShapeDtypeStruct((s_len, c), BF16),
        scratch_shapes=[pltpu.VMEM((ts + CONV_HALO, c), F32), pltpu.VMEM((ts, c), F32)],
        compiler_params=_params(("parallel",), 32),
        name="conv_block",
    )(h, h, conv_w, conv_b, ln_g, ln_b, out_g)


def _rotary_tables(seq):
    pos = jnp.arange(seq, dtype=F32)
    inv_freq = ROPE_THETA ** (-(jnp.arange(0, ROT_DIM, 2, dtype=F32) / ROT_DIM))
    ang = pos[:, None] * inv_freq[None, :]
    cos, sin = jnp.cos(ang), jnp.sin(ang)
    half = ROT_DIM // 2
    rest = HEAD_DIM - ROT_DIM
    ones = jnp.ones((seq, rest), F32)
    zeros_h = jnp.zeros((seq, half), F32)
    zeros_r = jnp.zeros((seq, rest), F32)
    cos_t = jnp.concatenate([cos, cos, ones], axis=-1)
    sin_lo = jnp.concatenate([-sin, zeros_h, zeros_r], axis=-1)
    sin_hi = jnp.concatenate([zeros_h, sin, zeros_r], axis=-1)
    return cos_t, sin_lo, sin_hi


def _ffn(x, norm_g, w_gate, w_up, w_down, layer):
    n = _rmsnorm(x, norm_g, layer, BF16)
    h = _dual_mm(n, w_gate, w_up, layer, 0, 0, D_FF, "swiglu", BF16)
    return _res_mm([h], w_down, layer, x, FFN_RESIDUAL_SCALE)


def kernel(x, ffn1_norm, ffn1_w_gate, ffn1_w_up, ffn1_w_down, mix_norm, w_in, conv_w, conv_b, conv_ln_g, conv_ln_b, attn_out_norm, conv_out_norm, w_out, ffn2_norm, ffn2_w_gate, ffn2_w_up, ffn2_w_down, final_norm):
    b, s_len, d = x.shape
    assert b == 1 and d == D_MODEL
    x = x.reshape(s_len, d)
    rot = _rotary_tables(s_len)
    a, c = ATTN_WIDTH, CONV_CHANNELS

    def vec(p):
        return p.reshape(p.shape[0], 1, p.shape[1])

    ffn1_norm, mix_norm, ffn2_norm = vec(ffn1_norm), vec(mix_norm), vec(ffn2_norm)
    conv_b, conv_ln_g, conv_ln_b = vec(conv_b), vec(conv_ln_g), vec(conv_ln_b)
    attn_out_norm, conv_out_norm = vec(attn_out_norm), vec(conv_out_norm)
    w1g, w1u, w1d = ffn1_w_gate.astype(BF16), ffn1_w_up.astype(BF16), ffn1_w_down.astype(BF16)
    w2g, w2u, w2d = ffn2_w_gate.astype(BF16), ffn2_w_up.astype(BF16), ffn2_w_down.astype(BF16)
    w_in_b, w_out_b = w_in.astype(BF16), w_out.astype(BF16)

    for l in range(DEPTH):
        x = _ffn(x, ffn1_norm, w1g, w1u, w1d, l)
        u = _rmsnorm(x, mix_norm, l, BF16)
        qkv = _qkv_mm(u, w_in_b, l, rot)
        glu = _dual_mm(u, w_in_b, w_in_b, l, 3 * a, 3 * a + c, c, "glu", F32)
        outs, lses = [], []
        for window, dilation in DILATED_BRANCHES:
            o, lse = _attn_branch(qkv, window, dilation)
            outs.append(o)
            lses.append(lse)
        attn_n = _attn_merge(outs, lses, attn_out_norm, l)
        conv_n = _conv_block(glu, conv_w, conv_b, conv_ln_g, conv_ln_b, conv_out_norm, l)
        x = _res_mm([attn_n, conv_n], w_out_b, l, x, 1.0)
        x = _ffn(x, ffn2_norm, w2g, w2u, w2d, l)
    out = _rmsnorm(x, final_norm.reshape(1, 1, d), 0, F32)
    return out.reshape(b, s_len, d)
```

```python
import functools

import jax
import jax.numpy as jnp
from jax import lax
from jax.experimental import pallas as pl
from jax.experimental.pallas import tpu as pltpu

D_MODEL = 4096
DEPTH = 4
N_HEADS = 16
HEAD_DIM = 128
ATTN_WIDTH = N_HEADS * HEAD_DIM
CONV_CHANNELS = D_MODEL - ATTN_WIDTH
IN_WIDTH = 3 * ATTN_WIDTH + 2 * CONV_CHANNELS
CONV_WIDTH = 31
D_FF = (3 * D_MODEL) // 2
ROPE_THETA = 500000.0
ROT_DIM = HEAD_DIM // 4
DILATED_BRANCHES = ((128, 1), (512, 4), (2048, 16))
BLOCK = 128
RMS_EPS = 1e-5
LN_EPS = 1e-5
FFN_RESIDUAL_SCALE = 0.5
MASK_VALUE = -1e30

LANES = 128
SUBLANES = 8
CONV_HALO = 32
ATTN_ROWS = BLOCK * max(d for _, d in DILATED_BRANCHES)
ATTN_HEADS_PER_STEP = 1
MIB = 1024 * 1024

F32 = jnp.float32
BF16 = jnp.bfloat16


def _params(semantics, vmem_mib):
    return pltpu.CompilerParams(dimension_semantics=semantics,
                                vmem_limit_bytes=vmem_mib * MIB)


def _rmsnorm_kernel(x_ref, g_ref, o_ref):
    x = x_ref[...]
    ms = jnp.mean(x * x, axis=-1, keepdims=True)
    o_ref[...] = (x * lax.rsqrt(ms + RMS_EPS) * g_ref[...]).astype(o_ref.dtype)


def _rmsnorm(x, g, layer, out_dtype, tr=256):
    m, d = x.shape
    return pl.pallas_call(
        _rmsnorm_kernel,
        grid=(m // tr,),
        in_specs=[pl.BlockSpec((tr, d), lambda i: (i, 0)),
                  pl.BlockSpec((None, 1, d), lambda i: (layer, 0, 0))],
        out_specs=pl.BlockSpec((tr, d), lambda i: (i, 0)),
        out_shape=jax.ShapeDtypeStruct((m, d), out_dtype),
        compiler_params=_params(("parallel",), 32),
        name="rmsnorm",
    )(x, g)


def _dual_mm_kernel(a_ref, w1_ref, w2_ref, o_ref, *, mode):
    a = a_ref[...]
    p1 = jnp.dot(a, w1_ref[...], preferred_element_type=F32)
    p2 = jnp.dot(a, w2_ref[...], preferred_element_type=F32)
    if mode == "swiglu":
        o = (p1 * jax.nn.sigmoid(p1)) * p2
    else:
        o = p1 * jax.nn.sigmoid(p2)
    o_ref[...] = o.astype(o_ref.dtype)


def _dual_mm(a, w1, w2, layer, off1, off2, n_out, mode, out_dtype, tm=1024, tn=512):
    m, k = a.shape
    o1, o2 = off1 // tn, off2 // tn
    return pl.pallas_call(
        functools.partial(_dual_mm_kernel, mode=mode),
        grid=(m // tm, n_out // tn),
        in_specs=[pl.BlockSpec((tm, k), lambda i, j: (i, 0)),
                  pl.BlockSpec((None, k, tn), lambda i, j: (layer, 0, o1 + j)),
                  pl.BlockSpec((None, k, tn), lambda i, j: (layer, 0, o2 + j))],
        out_specs=pl.BlockSpec((tm, tn), lambda i, j: (i, j)),
        out_shape=jax.ShapeDtypeStruct((m, n_out), out_dtype),
        compiler_params=_params(("parallel", "arbitrary"), 56),
        name="dual_mm_" + mode,
    )(a, w1, w2)


def _res_mm_kernel(*refs, n_a, scale):
    a_refs, w_refs = refs[:n_a], refs[n_a:2 * n_a]
    res_ref, o_ref = refs[2 * n_a], refs[2 * n_a + 1]
    acc = jnp.dot(a_refs[0][...], w_refs[0][...], preferred_element_type=F32)
    for a_ref, w_ref in zip(a_refs[1:], w_refs[1:]):
        acc = acc + jnp.dot(a_ref[...], w_ref[...], preferred_element_type=F32)
    if scale != 1.0:
        acc = scale * acc
    o_ref[...] = res_ref[...] + acc


def _res_mm(a_list, w, layer, res, scale, tm=1024, tn=512):
    n_a = len(a_list)
    m, k = a_list[0].shape
    n = w.shape[-1]
    a_specs = [pl.BlockSpec((tm, k), lambda i, j: (i, 0)) for _ in a_list]
    w_specs = [pl.BlockSpec((None, k, tn), functools.partial(lambda i, j, s: (layer, s, j), s=s))
               for s in range(n_a)]
    return pl.pallas_call(
        functools.partial(_res_mm_kernel, n_a=n_a, scale=scale),
        grid=(m // tm, n // tn),
        in_specs=a_specs + w_specs + [pl.BlockSpec((tm, tn), lambda i, j: (i, j))],
        out_specs=pl.BlockSpec((tm, tn), lambda i, j: (i, j)),
        out_shape=jax.ShapeDtypeStruct((m, n), F32),
        compiler_params=_params(("parallel", "arbitrary"), 56),
        name="res_mm",
    )(*a_list, *([w] * n_a), res)


def _qkv_kernel(a_ref, w_ref, cos_ref, sin_lo_ref, sin_hi_ref, o_ref, *, n_rot_tiles):
    p = jnp.dot(a_ref[...], w_ref[...], preferred_element_type=F32)
    j = pl.program_id(1)
    tn = p.shape[1]

    @pl.when(j < n_rot_tiles)
    def _():
        cos, sin_lo, sin_hi = cos_ref[...], sin_lo_ref[...], sin_hi_ref[...]
        for h in range(tn // HEAD_DIM):
            cols = slice(h * HEAD_DIM, (h + 1) * HEAD_DIM)
            x = p[:, cols]
            r = (x * cos + pltpu.roll(x, HEAD_DIM - ROT_DIM // 2, 1) * sin_lo
                 + pltpu.roll(x, ROT_DIM // 2, 1) * sin_hi)
            o_ref[:, cols] = r.astype(o_ref.dtype)

    @pl.when(j >= n_rot_tiles)
    def _():
        o_ref[...] = p.astype(o_ref.dtype)


def _qkv_mm(a, w, layer, rot, tm=1024, tn=512):
    m, k = a.shape
    n_out = 3 * ATTN_WIDTH
    cos, sin_lo, sin_hi = rot
    tab_spec = pl.BlockSpec((tm, HEAD_DIM), lambda i, j: (i, 0))
    return pl.pallas_call(
        functools.partial(_qkv_kernel, n_rot_tiles=2 * ATTN_WIDTH // tn),
        grid=(m // tm, n_out // tn),
        in_specs=[pl.BlockSpec((tm, k), lambda i, j: (i, 0)),
                  pl.BlockSpec((None, k, tn), lambda i, j: (layer, 0, j)),
                  tab_spec, tab_spec, tab_spec],
        out_specs=pl.BlockSpec((tm, tn), lambda i, j: (i, j)),
        out_shape=jax.ShapeDtypeStruct((m, n_out), F32),
        compiler_params=_params(("parallel", "arbitrary"), 56),
        name="qkv_mm",
    )(a, w, cos, sin_lo, sin_hi)


def _attn_tile(q, k, v, mask):
    s = lax.dot_general(q.astype(BF16), k.astype(BF16), (((1,), (1,)), ((), ())),
                        preferred_element_type=F32) * (HEAD_DIM ** -0.5)
    s = jnp.where(mask, s, MASK_VALUE)
    m = jnp.max(s, axis=-1, keepdims=True)
    p = jnp.exp(s - m)
    den = jnp.sum(p, axis=-1, keepdims=True)
    o = jnp.dot(p.astype(BF16), v.astype(BF16), preferred_element_type=F32)
    return o * (1.0 / den), m + jnp.log(den)


def _strided_rows(start, size, stride):
    if stride == 1:
        return pl.ds(start, size)
    return pl.ds(start, size, stride=stride)


def _attn_kernel(q_ref, kp_ref, kc_ref, vp_ref, vc_ref, out_ref, *bufs, branches, merge_chunk,
                 max_group):
    n_br = len(branches)
    obufs, lbufs = bufs[:n_br], bufs[n_br:]
    step = pl.program_id(0)
    rows_total, width = q_ref.shape
    qi = lax.broadcasted_iota(jnp.int32, (BLOCK, 2 * BLOCK), 0) + BLOCK
    kj = lax.broadcasted_iota(jnp.int32, (BLOCK, 2 * BLOCK), 1)
    delta = qi - kj

    assert width == HEAD_DIM

    def run_tiles(n_tiles, load, mask, bi):
        group = max(g for g in range(1, max_group + 1) if n_tiles % g == 0)

        def body(it, carry):
            loaded = [load(it * group + g) for g in range(group)]
            results = [_attn_tile(q, k, v, mask) for _, q, k, v in loaded]
            for (q_rows, _, _, _), (o, lse) in zip(loaded, results):
                obufs[bi][q_rows, :] = o
                lbufs[bi][q_rows, :] = jnp.broadcast_to(lse, (BLOCK, HEAD_DIM))
            return carry

        if n_tiles == group:
            body(0, 0)
        else:
            lax.fori_loop(0, n_tiles // group, body, 0)

    for bi, (window, d) in enumerate(branches):
        band = (delta >= 0) & (delta <= window // d)
        first_mask = band & ((kj >= BLOCK) | (step > 0))
        span = BLOCK * d
        nbk = rows_total // span

        def load_first(r, d=d, span=span, nbk=nbk):
            q_rows = _strided_rows(r, BLOCK, d)
            p_rows = _strided_rows(r + (nbk - 1) * span, BLOCK, d)
            k = jnp.concatenate([kp_ref[p_rows, :], kc_ref[q_rows, :]], axis=0)
            v = jnp.concatenate([vp_ref[p_rows, :], vc_ref[q_rows, :]], axis=0)
            return q_rows, q_ref[q_rows, :], k, v

        def load_later(idx, d=d, span=span):
            if d == 1:
                base = pl.multiple_of((idx + 1) * span, BLOCK)
            else:
                base = idx % d + (idx // d + 1) * span
            q_rows = _strided_rows(base, BLOCK, d)
            kv_rows = _strided_rows(base - span, 2 * BLOCK, d)
            return q_rows, q_ref[q_rows, :], kc_ref[kv_rows, :], vc_ref[kv_rows, :]

        run_tiles(d, load_first, first_mask, bi)
        if nbk > 1:
            run_tiles(d * (nbk - 1), load_later, band, bi)

    def merge(rc, carry):
        rows = pl.ds(pl.multiple_of(rc * merge_chunk, merge_chunk), merge_chunk)
        lses = [lbuf[rows, :] for lbuf in lbufs]
        mx = functools.reduce(jnp.maximum, lses)
        es = [jnp.exp(l - mx) for l in lses]
        inv = 1.0 / functools.reduce(lambda x, y: x + y, es)
        acc = (es[0] * inv) * obufs[0][rows, :]
        for bi in range(1, n_br):
            acc = acc + (es[bi] * inv) * obufs[bi][rows, :]
        out_ref[rows, :] = acc
        return carry

    lax.fori_loop(0, rows_total // merge_chunk, merge, 0, unroll=2)


def _attention(qkv):
    s_len = qkv.shape[0]
    assert s_len % ATTN_ROWS == 0
    a = ATTN_WIDTH
    hw = ATTN_HEADS_PER_STEP * HEAD_DIM
    ncol = a // hw
    n_br = len(DILATED_BRANCHES)

    def spec(part, prev):
        if prev:
            return pl.BlockSpec((ATTN_ROWS, hw),
                                lambda m, g: (jnp.maximum(m - 1, 0), part * ncol + g))
        return pl.BlockSpec((ATTN_ROWS, hw), lambda m, g: (m, part * ncol + g))

    return pl.pallas_call(
        functools.partial(_attn_kernel, branches=DILATED_BRANCHES, merge_chunk=32,
                          max_group=8),
        grid=(s_len // ATTN_ROWS, ncol),
        in_specs=[spec(0, False), spec(1, True), spec(1, False), spec(2, True), spec(2, False)],
        out_specs=pl.BlockSpec((ATTN_ROWS, hw), lambda m, g: (m, g)),
        out_shape=jax.ShapeDtypeStruct((s_len, a), F32),
        scratch_shapes=[pltpu.VMEM((ATTN_ROWS, hw), F32) for _ in range(2 * n_br)],
        compiler_params=_params(("parallel", "arbitrary"), 48),
        name="attention",
    )(qkv, qkv, qkv, qkv, qkv)


def _conv_kernel(h_ref, halo_ref, w_ref, b_ref, lng_ref, lnb_ref, g_ref, out_ref,
                 hbuf_ref, shift_ref, cbuf_ref, *, ts, row_chunk, ln_chunk):
    i = pl.program_id(0)
    halo = halo_ref[...]
    hbuf_ref[0:CONV_HALO, :] = jnp.where(i > 0, halo, jnp.zeros_like(halo))
    hbuf_ref[CONV_HALO:, :] = h_ref[...]
    first = CONV_HALO - (CONV_WIDTH - 1)
    ext = shift_ref.shape[1]

    def col_group(c, carry):
        cols = pl.ds(pl.multiple_of(c * LANES, LANES), LANES)
        for s in range(1, SUBLANES):
            shift_ref[s - 1, :, :] = hbuf_ref[pl.ds(s, ext), cols]
        w = w_ref[:, cols]
        b = b_ref[:, cols]
        for rc in range(ts // row_chunk):
            base = rc * row_chunk
            acc = None
            for j in range(CONV_WIDTH):
                off = first + j
                s, aligned = off % SUBLANES, off - off % SUBLANES
                if s == 0:
                    tap = hbuf_ref[pl.ds(base + aligned, row_chunk), cols]
                else:
                    tap = shift_ref[s - 1, pl.ds(base + aligned, row_chunk), :]
                term = w[j:j + 1, :] * tap
                acc = term if acc is None else acc + term
            cbuf_ref[pl.ds(base, row_chunk), cols] = acc + b
        return carry

    lax.fori_loop(0, CONV_CHANNELS // LANES, col_group, 0)

    def norm_rows(rc, carry):
        rows = pl.ds(pl.multiple_of(rc * ln_chunk, ln_chunk), ln_chunk)
        x = cbuf_ref[rows, :]
        mu = jnp.mean(x, axis=-1, keepdims=True)
        xc = x - mu
        var = jnp.mean(xc * xc, axis=-1, keepdims=True)
        y = xc * lax.rsqrt(var + LN_EPS) * lng_ref[...] + lnb_ref[...]
        y = y * jax.nn.sigmoid(y)
        ms = jnp.mean(y * y, axis=-1, keepdims=True)
        out_ref[rows, :] = (y * lax.rsqrt(ms + RMS_EPS) * g_ref[...]).astype(out_ref.dtype)
        return carry

    lax.fori_loop(0, ts // ln_chunk, norm_rows, 0, unroll=2)


def _conv_block(h, conv_w, conv_b, ln_g, ln_b, out_g, layer, ts=256):
    s_len, c = h.shape
    vec_spec = pl.BlockSpec((None, 1, c), lambda i: (layer, 0, 0))
    halo_per_block = ts // CONV_HALO
    return pl.pallas_call(
        functools.partial(_conv_kernel, ts=ts, row_chunk=64, ln_chunk=16),
        grid=(s_len // ts,),
        in_specs=[pl.BlockSpec((ts, c), lambda i: (i, 0)),
                  pl.BlockSpec((CONV_HALO, c), lambda i: (jnp.maximum(i * halo_per_block - 1, 0), 0)),
                  pl.BlockSpec((None, CONV_WIDTH, c), lambda i: (layer, 0, 0)),
                  vec_spec, vec_spec, vec_spec, vec_spec],
        out_specs=pl.BlockSpec((ts, c), lambda i: (i, 0)),
        out_shape=jax.ShapeDtypeStruct((s_len, c), BF16),
        scratch_shapes=[pltpu.VMEM((ts + CONV_HALO, c), F32),
                        pltpu.VMEM((SUBLANES - 1, ts + CONV_HALO - SUBLANES, LANES), F32),
                        pltpu.VMEM((ts, c), F32)],
        compiler_params=_params(("parallel",), 32),
        name="conv_block",
    )(h, h, conv_w, conv_b, ln_g, ln_b, out_g)


def _rotary_tables(seq):
    pos = jnp.arange(seq, dtype=F32)
    inv_freq = ROPE_THETA ** (-(jnp.arange(0, ROT_DIM, 2, dtype=F32) / ROT_DIM))
    ang = pos[:, None] * inv_freq[None, :]
    cos, sin = jnp.cos(ang), jnp.sin(ang)
    half = ROT_DIM // 2
    rest = HEAD_DIM - ROT_DIM
    ones = jnp.ones((seq, rest), F32)
    zeros_h = jnp.zeros((seq, half), F32)
    zeros_r = jnp.zeros((seq, rest), F32)
    cos_t = jnp.concatenate([cos, cos, ones], axis=-1)
    sin_lo = jnp.concatenate([-sin, zeros_h, zeros_r], axis=-1)
    sin_hi = jnp.concatenate([zeros_h, sin, zeros_r], axis=-1)
    return cos_t, sin_lo, sin_hi


def _ffn(x, norm_g, w_gate, w_up, w_down, layer):
    n = _rmsnorm(x, norm_g, layer, BF16)
    h = _dual_mm(n, w_gate, w_up, layer, 0, 0, D_FF, "swiglu", BF16)
    return _res_mm([h], w_down, layer, x, FFN_RESIDUAL_SCALE)


def kernel(x, ffn1_norm, ffn1_w_gate, ffn1_w_up, ffn1_w_down, mix_norm, w_in, conv_w, conv_b, conv_ln_g, conv_ln_b, attn_out_norm, conv_out_norm, w_out, ffn2_norm, ffn2_w_gate, ffn2_w_up, ffn2_w_down, final_norm):
    b, s_len, d = x.shape
    assert b == 1 and d == D_MODEL
    x = x.reshape(s_len, d)
    rot = _rotary_tables(s_len)
    a, c = ATTN_WIDTH, CONV_CHANNELS

    def vec(p):
        return p.reshape(p.shape[0], 1, p.shape[1])

    ffn1_norm, mix_norm, ffn2_norm = vec(ffn1_norm), vec(mix_norm), vec(ffn2_norm)
    conv_b, conv_ln_g, conv_ln_b = vec(conv_b), vec(conv_ln_g), vec(conv_ln_b)
    attn_out_norm, conv_out_norm = vec(attn_out_norm), vec(conv_out_norm)
    w1g, w1u, w1d = ffn1_w_gate.astype(BF16), ffn1_w_up.astype(BF16), ffn1_w_down.astype(BF16)
    w2g, w2u, w2d = ffn2_w_gate.astype(BF16), ffn2_w_up.astype(BF16), ffn2_w_down.astype(BF16)
    w_in_b, w_out_b = w_in.astype(BF16), w_out.astype(BF16)

    for l in range(DEPTH):
        x = _ffn(x, ffn1_norm, w1g, w1u, w1d, l)
        u = _rmsnorm(x, mix_norm, l, BF16)
        qkv = _qkv_mm(u, w_in_b, l, rot)
        glu = _dual_mm(u, w_in_b, w_in_b, l, 3 * a, 3 * a + c, c, "glu", F32)
        attn_n = _rmsnorm(_attention(qkv), attn_out_norm, l, BF16)
        conv_n = _conv_block(glu, conv_w, conv_b, conv_ln_g, conv_ln_b, conv_out_norm, l)
        x = _res_mm([attn_n, conv_n], w_out_b, l, x, 1.0)
        x = _ffn(x, ffn2_norm, w2g, w2u, w2d, l)
    out = _rmsnorm(x, final_norm.reshape(1, 1, d), 0, F32)
    return out.reshape(b, s_len, d)
```

```python
import functools
from typing import Any, Callable, NamedTuple

import jax
import jax.numpy as jnp
from jax import lax
from jax.experimental import pallas as pl
from jax.experimental.pallas import tpu as pltpu

D_MODEL = 4096
DEPTH = 4
N_HEADS = 16
HEAD_DIM = 128
ATTN_WIDTH = N_HEADS * HEAD_DIM
CONV_CHANNELS = D_MODEL - ATTN_WIDTH
IN_WIDTH = 3 * ATTN_WIDTH + 2 * CONV_CHANNELS
CONV_WIDTH = 31
D_FF = (3 * D_MODEL) // 2
ROPE_THETA = 500000.0
ROT_DIM = HEAD_DIM // 4
DILATED_BRANCHES = ((128, 1), (512, 4), (2048, 16))
BLOCK = 128
RMS_EPS = 1e-5
LN_EPS = 1e-5
FFN_RESIDUAL_SCALE = 0.5
MASK_VALUE = -1e30

LANES = 128
SUBLANES = 8
CONV_HALO = 32
ATTN_ROWS = BLOCK * max(d for _, d in DILATED_BRANCHES)
MIB = 1024 * 1024

F32 = jnp.float32
BF16 = jnp.bfloat16


def _params(semantics, vmem_mib):
    return pltpu.CompilerParams(dimension_semantics=semantics,
                                vmem_limit_bytes=vmem_mib * MIB)


def _rmsnorm_kernel(x_ref, g_ref, o_ref):
    x = x_ref[...]
    ms = jnp.mean(x * x, axis=-1, keepdims=True)
    o_ref[...] = (x * lax.rsqrt(ms + RMS_EPS) * g_ref[...]).astype(o_ref.dtype)


def _rmsnorm(x, g, layer, out_dtype, tr=256):
    m, d = x.shape
    return pl.pallas_call(
        _rmsnorm_kernel,
        grid=(m // tr,),
        in_specs=[pl.BlockSpec((tr, d), lambda i: (i, 0)),
                  pl.BlockSpec((None, 1, d), lambda i: (layer, 0, 0))],
        out_specs=pl.BlockSpec((tr, d), lambda i: (i, 0)),
        out_shape=jax.ShapeDtypeStruct((m, d), out_dtype),
        compiler_params=_params(("parallel",), 32),
        name="rmsnorm",
    )(x, g)


def _prenorm_kernel(x_ref, g_ref, xg_ref, rs_ref):
    x = x_ref[...]
    ms = jnp.mean(x * x, axis=-1, keepdims=True)
    xg_ref[...] = (x * g_ref[...]).astype(xg_ref.dtype)
    rs_ref[...] = jnp.broadcast_to(lax.rsqrt(ms + RMS_EPS), rs_ref.shape)


def _prenorm(x, g, layer, tr=256):
    m, d = x.shape
    return pl.pallas_call(
        _prenorm_kernel,
        grid=(m // tr,),
        in_specs=[pl.BlockSpec((tr, d), lambda i: (i, 0)),
                  pl.BlockSpec((None, 1, d), lambda i: (layer, 0, 0))],
        out_specs=[pl.BlockSpec((tr, d), lambda i: (i, 0)),
                   pl.BlockSpec((tr, LANES), lambda i: (i, 0))],
        out_shape=[jax.ShapeDtypeStruct((m, d), BF16), jax.ShapeDtypeStruct((m, LANES), F32)],
        compiler_params=_params(("parallel",), 32),
        name="prenorm",
    )(x, g)


def _cast_plan(w, layer, block, n_steps_j):
    _, rows, cols = w.shape
    br, bc = block
    nbc = cols // bc
    n_blocks = (rows // br) * nbc

    def block_index(i, j):
        s = jnp.minimum(i * n_steps_j + j, n_blocks - 1)
        return s // nbc, s % nbc

    in_spec = pl.BlockSpec((None, br, bc), lambda i, j: (layer,) + block_index(i, j))
    out_spec = pl.BlockSpec((br, bc), block_index)
    return in_spec, out_spec, jax.ShapeDtypeStruct((rows, cols), BF16), n_blocks


def _run_casts(src_refs, dst_refs, n_blocks_list, n_steps):
    step = pl.program_id(0) * pl.num_programs(1) + pl.program_id(1)
    for src, dst, n_blocks in zip(src_refs, dst_refs, n_blocks_list):
        if n_blocks == n_steps:
            dst[...] = src[...].astype(dst.dtype)
        else:
            @pl.when(step < n_blocks)
            def _(src=src, dst=dst):
                dst[...] = src[...].astype(dst.dtype)


def _host_call(body, grid, in_specs, out_specs, out_shapes, operands, casts, vmem_mib, name,
               scratch_shapes=()):
    n_in, n_out, n_cast = len(in_specs), len(out_specs), len(casts)
    plans = [_cast_plan(w, layer, block, grid[1]) for w, layer, block in casts]
    n_steps = grid[0] * grid[1]
    assert all(p[3] <= n_steps for p in plans)
    n_blocks_list = [p[3] for p in plans]

    def kernel_fn(*refs):
        ins, cast_ins = refs[:n_in], refs[n_in:n_in + n_cast]
        outs = refs[n_in + n_cast:n_in + n_cast + n_out]
        cast_outs = refs[n_in + n_cast + n_out:n_in + 2 * n_cast + n_out]
        scratch = refs[n_in + 2 * n_cast + n_out:]
        _run_casts(cast_ins, cast_outs, n_blocks_list, n_steps)
        body(ins, outs, scratch)

    results = pl.pallas_call(
        kernel_fn,
        grid=grid,
        in_specs=list(in_specs) + [p[0] for p in plans],
        out_specs=list(out_specs) + [p[1] for p in plans],
        out_shape=list(out_shapes) + [p[2] for p in plans],
        scratch_shapes=list(scratch_shapes),
        compiler_params=_params(("arbitrary", "arbitrary"), vmem_mib),
        name=name,
    )(*operands, *[w for w, _, _ in casts])
    return results[:n_out], results[n_out:]


def _lane_tiles(width):
    return [slice(c * LANES, (c + 1) * LANES) for c in range(width // LANES)]


def _dual_mm(a, rs, w1, w2, off1, off2, n_out, mode, out_dtype, casts=(), tm=1024, tn=512):
    m, k = a.shape
    o1, o2 = off1 // tn, off2 // tn

    def body(ins, outs, scratch):
        a_ref, rs_ref, w1_ref, w2_ref = ins
        o_ref, = outs
        x = a_ref[...]
        p1 = jnp.dot(x, w1_ref[...], preferred_element_type=F32)
        p2 = jnp.dot(x, w2_ref[...], preferred_element_type=F32)
        row_scale = rs_ref[...]
        for cols in _lane_tiles(tn):
            g1, g2 = p1[:, cols] * row_scale, p2[:, cols] * row_scale
            if mode == "swiglu":
                o = (g1 * jax.nn.sigmoid(g1)) * g2
            else:
                o = g1 * jax.nn.sigmoid(g2)
            o_ref[:, cols] = o.astype(o_ref.dtype)

    (out,), cast_outs = _host_call(
        body, (m // tm, n_out // tn),
        [pl.BlockSpec((tm, k), lambda i, j: (i, 0)),
         pl.BlockSpec((tm, LANES), lambda i, j: (i, 0)),
         pl.BlockSpec((k, tn), lambda i, j: (0, o1 + j)),
         pl.BlockSpec((k, tn), lambda i, j: (0, o2 + j))],
        [pl.BlockSpec((tm, tn), lambda i, j: (i, j))],
        [jax.ShapeDtypeStruct((m, n_out), out_dtype)],
        (a, rs, w1, w2), list(casts), 60, "dual_mm_" + mode)
    return out, cast_outs


def _res_mm(a_list, w, res, scale, next_norm=None, tm=1024, tn=512):
    n_a = len(a_list)
    m, k = a_list[0].shape
    n = w.shape[-1]
    n_j = n // tn

    def body(ins, outs, scratch):
        a_refs, w_refs, res_ref = ins[:n_a], ins[n_a:2 * n_a], ins[2 * n_a]
        acc = jnp.dot(a_refs[0][...], w_refs[0][...], preferred_element_type=F32)
        for a_ref, w_ref in zip(a_refs[1:], w_refs[1:]):
            acc = acc + jnp.dot(a_ref[...], w_ref[...], preferred_element_type=F32)
        if scale != 1.0:
            acc = scale * acc
        x = res_ref[...] + acc
        outs[0][...] = x
        if next_norm is None:
            return
        g_ref = ins[2 * n_a + 1]
        xg_ref, rs_ref = outs[1], outs[2]
        ssq_ref, = scratch
        j = pl.program_id(1)
        xg_ref[...] = (x * g_ref[...]).astype(xg_ref.dtype)
        part = None
        for cols in _lane_tiles(tn):
            sq = x[:, cols] * x[:, cols]
            part = sq if part is None else part + sq

        @pl.when(j == 0)
        def _():
            ssq_ref[...] = part

        @pl.when(j > 0)
        def _():
            ssq_ref[...] += part

        @pl.when(j == n_j - 1)
        def _():
            ms = jnp.sum(ssq_ref[...], axis=-1, keepdims=True) * (1.0 / n)
            rs_ref[...] = jnp.broadcast_to(lax.rsqrt(ms + RMS_EPS), rs_ref.shape)

    in_specs = ([pl.BlockSpec((tm, k), lambda i, j: (i, 0)) for _ in a_list]
                + [pl.BlockSpec((k, tn), functools.partial(lambda i, j, s: (s, j), s=s))
                   for s in range(n_a)]
                + [pl.BlockSpec((tm, tn), lambda i, j: (i, j))])
    out_specs = [pl.BlockSpec((tm, tn), lambda i, j: (i, j))]
    out_shapes = [jax.ShapeDtypeStruct((m, n), F32)]
    operands = list(a_list) + [w] * n_a + [res]
    scratch_shapes = []
    if next_norm is not None:
        g, layer = next_norm
        in_specs.append(pl.BlockSpec((None, 1, tn), lambda i, j: (layer, 0, j)))
        operands.append(g)
        out_specs += [pl.BlockSpec((tm, tn), lambda i, j: (i, j)),
                      pl.BlockSpec((tm, LANES), lambda i, j: (i, 0))]
        out_shapes += [jax.ShapeDtypeStruct((m, n), BF16), jax.ShapeDtypeStruct((m, LANES), F32)]
        scratch_shapes.append(pltpu.VMEM((tm, LANES), F32))
    outs, _ = _host_call(body, (m // tm, n_j), in_specs, out_specs, out_shapes, operands, [],
                         60, "res_mm", scratch_shapes)
    return outs


def _qkv_mm(a, rs, w, rot, casts=(), tm=1024, tn=512):
    m, k = a.shape
    n_out = 3 * ATTN_WIDTH
    n_rot_tiles = 2 * ATTN_WIDTH // tn
    tab_spec = pl.BlockSpec((tm, HEAD_DIM), lambda i, j: (i, 0))

    def body(ins, outs, scratch):
        a_ref, rs_ref, w_ref, cos_ref, sin_lo_ref, sin_hi_ref = ins
        o_ref, = outs
        p = jnp.dot(a_ref[...], w_ref[...], preferred_element_type=F32)
        j = pl.program_id(1)
        row_scale = rs_ref[...]

        @pl.when(j < n_rot_tiles)
        def _():
            cos, sin_lo, sin_hi = cos_ref[...], sin_lo_ref[...], sin_hi_ref[...]
            for cols in _lane_tiles(tn):
                x = p[:, cols] * row_scale
                o_ref[:, cols] = (x * cos + pltpu.roll(x, HEAD_DIM - ROT_DIM // 2, 1) * sin_lo
                                  + pltpu.roll(x, ROT_DIM // 2, 1) * sin_hi)

        @pl.when(j >= n_rot_tiles)
        def _():
            for cols in _lane_tiles(tn):
                o_ref[:, cols] = p[:, cols] * row_scale

    (out,), cast_outs = _host_call(
        body, (m // tm, n_out // tn),
        [pl.BlockSpec((tm, k), lambda i, j: (i, 0)),
         pl.BlockSpec((tm, LANES), lambda i, j: (i, 0)),
         pl.BlockSpec((k, tn), lambda i, j: (0, j)),
         tab_spec, tab_spec, tab_spec],
        [pl.BlockSpec((tm, tn), lambda i, j: (i, j))],
        [jax.ShapeDtypeStruct((m, n_out), F32)],
        (a, rs, w) + tuple(rot), list(casts), 60, "qkv_mm")
    return out, cast_outs


class _Tile(NamedTuple):
    sp_rows: Any
    q: Callable
    k: Callable
    v: Callable
    dst_rows: Any


class _TileKind(NamedTuple):
    n: int
    branch: int
    mask: Any
    tile: Callable


def _aligned(x, multiple):
    return x if isinstance(x, int) else pl.multiple_of(x, multiple)


def _for_groups(n, max_group, fn):
    group = max(g for g in range(1, max_group + 1) if n % g == 0)
    if n == group:
        fn(list(range(group)))
        return

    def body(it, carry):
        fn([it * group + g for g in range(group)])
        return carry

    lax.fori_loop(0, n // group, body, 0)


def _attn_kernel(q_ref, kc_ref, vc_ref, out_ref, qt_ref, kt_ref, vt_ref, pkt_ref, pvt_ref,
                 pk_tail_ref, pv_tail_ref, s_ref, p_ref, *merge_bufs, branches, max_group,
                 merge_chunk):
    (w1, d1), (w2, d2), (w3, d3) = branches
    rows_total, width = q_ref.shape
    assert width == HEAD_DIM and d1 == 1 and d3 % d2 == 0 and rows_total == BLOCK * d3
    e = d3 // d2
    res_rows = rows_total // d2
    nb1 = rows_total // BLOCK
    nb2 = res_rows // BLOCK
    nums, maxs, dens = merge_bufs[0::3], merge_bufs[1::3], merge_bufs[2::3]
    step = pl.program_id(1)

    @pl.when(step == 0)
    def _():
        for ref in (pkt_ref, pvt_ref, pk_tail_ref, pv_tail_ref):
            ref[...] = jnp.zeros(ref.shape, ref.dtype)

    for r2 in range(d2):
        src = pl.ds(r2, res_rows, stride=d2)
        dst = pl.ds(r2 * res_rows, res_rows)
        qt_ref[dst, :] = q_ref[src, :]
        kt_ref[dst, :] = kc_ref[src, :]
        vt_ref[dst, :] = vc_ref[src, :]

    qi = lax.broadcasted_iota(jnp.int32, (BLOCK, 2 * BLOCK), 0) + BLOCK
    kj = lax.broadcasted_iota(jnp.int32, (BLOCK, 2 * BLOCK), 1)
    delta = qi - kj
    has_prev = (kj >= BLOCK) | (step > 0)

    def band(window, d):
        return (delta >= 0) & (delta <= window // d)

    def sp_rows(t):
        return pl.ds(_aligned(t * BLOCK, BLOCK), BLOCK)

    def cat(prev_ref, prev_rows, cur_ref, cur_rows):
        return lambda: jnp.concatenate([prev_ref[prev_rows, :], cur_ref[cur_rows, :]], axis=0)

    def b1_first(idx):
        rows = pl.ds(0, BLOCK)
        return _Tile(sp_rows(0), lambda: q_ref[rows, :],
                     cat(pk_tail_ref, slice(None), kc_ref, rows),
                     cat(pv_tail_ref, slice(None), vc_ref, rows), rows)

    def b1_later(idx):
        base = _aligned((idx + 1) * BLOCK, BLOCK)
        rows, kv_rows = pl.ds(base, BLOCK), pl.ds(base - BLOCK, 2 * BLOCK)
        return _Tile(sp_rows(idx + 1), lambda: q_ref[rows, :],
                     lambda: kc_ref[kv_rows, :], lambda: vc_ref[kv_rows, :], rows)

    def b2_first(r2):
        base = _aligned(r2 * res_rows, BLOCK)
        rows, prev_rows = pl.ds(base, BLOCK), pl.ds(base + res_rows - BLOCK, BLOCK)
        return _Tile(sp_rows(nb1 + r2 * nb2), lambda: qt_ref[rows, :],
                     cat(pkt_ref, prev_rows, kt_ref, rows),
                     cat(pvt_ref, prev_rows, vt_ref, rows), rows)

    def b2_later(idx):
        r2, b = idx % d2, idx // d2 + 1
        base = _aligned(r2 * res_rows + b * BLOCK, BLOCK)
        rows, kv_rows = pl.ds(base, BLOCK), pl.ds(base - BLOCK, 2 * BLOCK)
        return _Tile(sp_rows(nb1 + r2 * nb2 + b), lambda: qt_ref[rows, :],
                     lambda: kt_ref[kv_rows, :], lambda: vt_ref[kv_rows, :], rows)

    def b3_all(idx):
        r2, off = idx % d2, idx // d2
        rows = pl.ds(r2 * res_rows + off, BLOCK, stride=e)
        return _Tile(sp_rows(nb1 + d2 * nb2 + idx), lambda: qt_ref[rows, :],
                     cat(pkt_ref, rows, kt_ref, rows), cat(pvt_ref, rows, vt_ref, rows), rows)

    kinds = [
        _TileKind(1, 0, band(w1, d1) & has_prev, b1_first),
        _TileKind(nb1 - 1, 0, band(w1, d1), b1_later),
        _TileKind(d2, 1, band(w2, d2) & has_prev, b2_first),
        _TileKind(d2 * (nb2 - 1), 1, band(w2, d2), b2_later),
        _TileKind(d3, 2, band(w3, d3) & has_prev, b3_all),
    ]
    scale = HEAD_DIM ** -0.5

    for kind in kinds:
        def scores(idxs, kind=kind):
            tiles = [kind.tile(i) for i in idxs]
            qs = [t.q().astype(BF16) for t in tiles]
            ks = [t.k().astype(BF16) for t in tiles]
            ss = [lax.dot_general(q, k, (((1,), (1,)), ((), ())), preferred_element_type=F32)
                  for q, k in zip(qs, ks)]
            for t, s in zip(tiles, ss):
                s_ref[t.sp_rows, :] = jnp.where(kind.mask, s * scale, MASK_VALUE)
        _for_groups(kind.n, max_group, scores)

    for kind in kinds:
        def softmax(idxs, kind=kind):
            tiles = [kind.tile(i) for i in idxs]
            ss = [s_ref[t.sp_rows, :] for t in tiles]
            ms = [jnp.max(s, axis=-1, keepdims=True) for s in ss]
            ps = [jnp.exp(s - m) for s, m in zip(ss, ms)]
            for t, p, m in zip(tiles, ps, ms):
                p_ref[t.sp_rows, :] = p.astype(p_ref.dtype)
                maxs[kind.branch][t.dst_rows, :] = jnp.broadcast_to(m, (BLOCK, HEAD_DIM))
        _for_groups(kind.n, max_group, softmax)

    ones = jnp.ones((2 * BLOCK, HEAD_DIM), BF16)
    for kind in kinds:
        def values(idxs, kind=kind):
            tiles = [kind.tile(i) for i in idxs]
            vs = [jnp.concatenate([t.v().astype(BF16), ones], axis=1) for t in tiles]
            os = [jnp.dot(p_ref[t.sp_rows, :], v, preferred_element_type=F32)
                  for t, v in zip(tiles, vs)]
            for t, o in zip(tiles, os):
                nums[kind.branch][t.dst_rows, :] = o[:, :HEAD_DIM]
                dens[kind.branch][t.dst_rows, :] = o[:, HEAD_DIM:]
        _for_groups(kind.n, max_group, values)

    def merge(it, carry):
        for r2 in range(d2):
            res = pl.ds(pl.multiple_of(r2 * res_rows + it * merge_chunk, merge_chunk), merge_chunk)
            nat = pl.ds(r2 + it * merge_chunk * d2, merge_chunk, stride=d2)
            rows = (nat, res, res)
            ms = [maxs[b][rows[b], :] for b in range(3)]
            top = jnp.maximum(jnp.maximum(ms[0], ms[1]), ms[2])
            ws = [jnp.exp(m - top) for m in ms]
            numer = ws[0] * nums[0][rows[0], :]
            denom = ws[0] * dens[0][rows[0], :]
            for b in (1, 2):
                numer = numer + ws[b] * nums[b][rows[b], :]
                denom = denom + ws[b] * dens[b][rows[b], :]
            out_ref[nat, :] = numer * (1.0 / denom)
        return carry

    lax.fori_loop(0, res_rows // merge_chunk, merge, 0)

    pkt_ref[...] = kt_ref[...]
    pvt_ref[...] = vt_ref[...]
    pk_tail_ref[...] = kc_ref[pl.ds(rows_total - BLOCK, BLOCK), :]
    pv_tail_ref[...] = vc_ref[pl.ds(rows_total - BLOCK, BLOCK), :]


def _attention(qkv):
    s_len = qkv.shape[0]
    assert s_len % ATTN_ROWS == 0 and len(DILATED_BRANCHES) == 3
    n_tiles = sum(ATTN_ROWS // BLOCK for _ in DILATED_BRANCHES)
    blk = (ATTN_ROWS, HEAD_DIM)

    def spec(part):
        return pl.BlockSpec(blk, lambda h, m: (m, part * N_HEADS + h))

    return pl.pallas_call(
        functools.partial(_attn_kernel, branches=DILATED_BRANCHES, max_group=16, merge_chunk=32),
        grid=(N_HEADS, s_len // ATTN_ROWS),
        in_specs=[spec(0), spec(1), spec(2)],
        out_specs=pl.BlockSpec(blk, lambda h, m: (m, h)),
        out_shape=jax.ShapeDtypeStruct((s_len, ATTN_WIDTH), F32),
        scratch_shapes=([pltpu.VMEM(blk, F32) for _ in range(5)]
                        + [pltpu.VMEM((BLOCK, HEAD_DIM), F32) for _ in range(2)]
                        + [pltpu.VMEM((n_tiles * BLOCK, 2 * BLOCK), F32),
                           pltpu.VMEM((n_tiles * BLOCK, 2 * BLOCK), BF16)]
                        + [pltpu.VMEM(blk, F32) for _ in range(9)]),
        compiler_params=_params(("arbitrary", "arbitrary"), 48),
        name="attention",
    )(qkv, qkv, qkv)


def _conv_kernel(h_ref, halo_ref, w_ref, b_ref, lng_ref, lnb_ref, g_ref, out_ref,
                 hbuf_ref, shift_ref, cbuf_ref, *, ts, row_chunk, ln_chunk):
    i = pl.program_id(0)
    halo = halo_ref[...]
    hbuf_ref[0:CONV_HALO, :] = jnp.where(i > 0, halo, jnp.zeros_like(halo))
    hbuf_ref[CONV_HALO:, :] = h_ref[...]
    first = CONV_HALO - (CONV_WIDTH - 1)
    ext = shift_ref.shape[1]

    def col_group(c, carry):
        cols = pl.ds(pl.multiple_of(c * LANES, LANES), LANES)
        for s in range(1, SUBLANES):
            shift_ref[s - 1, :, :] = hbuf_ref[pl.ds(s, ext), cols]
        w = w_ref[:, cols]
        b = b_ref[:, cols]
        for rc in range(ts // row_chunk):
            base = rc * row_chunk
            acc = None
            for j in range(CONV_WIDTH):
                off = first + j
                s, aligned = off % SUBLANES, off - off % SUBLANES
                if s == 0:
                    tap = hbuf_ref[pl.ds(base + aligned, row_chunk), cols]
                else:
                    tap = shift_ref[s - 1, pl.ds(base + aligned, row_chunk), :]
                term = w[j:j + 1, :] * tap
                acc = term if acc is None else acc + term
            cbuf_ref[pl.ds(base, row_chunk), cols] = acc + b
        return carry

    lax.fori_loop(0, CONV_CHANNELS // LANES, col_group, 0)

    def norm_rows(rc, carry):
        rows = pl.ds(pl.multiple_of(rc * ln_chunk, ln_chunk), ln_chunk)
        x = cbuf_ref[rows, :]
        mu = jnp.mean(x, axis=-1, keepdims=True)
        xc = x - mu
        var = jnp.mean(xc * xc, axis=-1, keepdims=True)
        y = xc * lax.rsqrt(var + LN_EPS) * lng_ref[...] + lnb_ref[...]
        y = y * jax.nn.sigmoid(y)
        ms = jnp.mean(y * y, axis=-1, keepdims=True)
        out_ref[rows, :] = (y * lax.rsqrt(ms + RMS_EPS) * g_ref[...]).astype(out_ref.dtype)
        return carry

    lax.fori_loop(0, ts // ln_chunk, norm_rows, 0, unroll=4)


def _conv_block(h, conv_w, conv_b, ln_g, ln_b, out_g, layer, ts=256):
    s_len, c = h.shape
    vec_spec = pl.BlockSpec((None, 1, c), lambda i: (layer, 0, 0))
    halo_per_block = ts // CONV_HALO
    return pl.pallas_call(
        functools.partial(_conv_kernel, ts=ts, row_chunk=64, ln_chunk=16),
        grid=(s_len // ts,),
        in_specs=[pl.BlockSpec((ts, c), lambda i: (i, 0)),
                  pl.BlockSpec((CONV_HALO, c), lambda i: (jnp.maximum(i * halo_per_block - 1, 0), 0)),
                  pl.BlockSpec((None, CONV_WIDTH, c), lambda i: (layer, 0, 0)),
                  vec_spec, vec_spec, vec_spec, vec_spec],
        out_specs=pl.BlockSpec((ts, c), lambda i: (i, 0)),
        out_shape=jax.ShapeDtypeStruct((s_len, c), BF16),
        scratch_shapes=[pltpu.VMEM((ts + CONV_HALO, c), F32),
                        pltpu.VMEM((SUBLANES - 1, ts + CONV_HALO - SUBLANES, LANES), F32),
                        pltpu.VMEM((ts, c), F32)],
        compiler_params=_params(("parallel",), 32),
        name="conv_block",
    )(h, h, conv_w, conv_b, ln_g, ln_b, out_g)


def _rotary_tables(seq):
    pos = jnp.arange(seq, dtype=F32)
    inv_freq = ROPE_THETA ** (-(jnp.arange(0, ROT_DIM, 2, dtype=F32) / ROT_DIM))
    ang = pos[:, None] * inv_freq[None, :]
    cos, sin = jnp.cos(ang), jnp.sin(ang)
    half = ROT_DIM // 2
    rest = HEAD_DIM - ROT_DIM
    ones = jnp.ones((seq, rest), F32)
    zeros_h = jnp.zeros((seq, half), F32)
    zeros_r = jnp.zeros((seq, rest), F32)
    cos_t = jnp.concatenate([cos, cos, ones], axis=-1)
    sin_lo = jnp.concatenate([-sin, zeros_h, zeros_r], axis=-1)
    sin_hi = jnp.concatenate([zeros_h, sin, zeros_r], axis=-1)
    return cos_t, sin_lo, sin_hi


def _cast_block(w, n_steps):
    _, rows, cols = w.shape
    for br, bc in ((512, 512), (512, 1024), (1024, 1024)):
        if rows % br == 0 and cols % bc == 0 and (rows // br) * (cols // bc) <= n_steps:
            return br, bc
    raise ValueError("no cast block for %s in %d steps" % (w.shape, n_steps))


def kernel(x, ffn1_norm, ffn1_w_gate, ffn1_w_up, ffn1_w_down, mix_norm, w_in, conv_w, conv_b, conv_ln_g, conv_ln_b, attn_out_norm, conv_out_norm, w_out, ffn2_norm, ffn2_w_gate, ffn2_w_up, ffn2_w_down, final_norm):
    b, s_len, d = x.shape
    assert b == 1 and d == D_MODEL
    x = x.reshape(s_len, d)
    rot = _rotary_tables(s_len)
    a, c = ATTN_WIDTH, CONV_CHANNELS

    def vec(p):
        return p.reshape(p.shape[0], 1, p.shape[1])

    ffn1_norm, mix_norm, ffn2_norm = vec(ffn1_norm), vec(mix_norm), vec(ffn2_norm)
    conv_b, conv_ln_g, conv_ln_b = vec(conv_b), vec(conv_ln_g), vec(conv_ln_b)
    attn_out_norm, conv_out_norm = vec(attn_out_norm), vec(conv_out_norm)
    host_steps = (s_len // 1024) * (D_FF // 512)

    def cast(w, layer):
        return w, layer, _cast_block(w, host_steps)

    gate_b, up_b = ffn1_w_gate[0].astype(BF16), ffn1_w_up[0].astype(BF16)
    xg, rs = _prenorm(x, ffn1_norm, 0)
    for l in range(DEPTH):
        last = l + 1 == DEPTH
        h, (down_b, w_in_b, w_out_b) = _dual_mm(
            xg, rs, gate_b, up_b, 0, 0, D_FF, "swiglu", BF16,
            casts=[cast(ffn1_w_down, l), cast(w_in, l), cast(w_out, l)])
        x, xg, rs = _res_mm([h], down_b, x, FFN_RESIDUAL_SCALE, next_norm=(mix_norm, l))
        qkv, (gate_b, up_b) = _qkv_mm(xg, rs, w_in_b, rot,
                                      casts=[cast(ffn2_w_gate, l), cast(ffn2_w_up, l)])
        glu, _ = _dual_mm(xg, rs, w_in_b, w_in_b, 3 * a, 3 * a + c, c, "glu", F32)
        attn_n = _rmsnorm(_attention(qkv), attn_out_norm, l, BF16)
        conv_n = _conv_block(glu, conv_w, conv_b, conv_ln_g, conv_ln_b, conv_out_norm, l)
        x, xg, rs = _res_mm([attn_n, conv_n], w_out_b, x, 1.0, next_norm=(ffn2_norm, l))
        next_casts = [] if last else [cast(ffn1_w_gate, l + 1), cast(ffn1_w_up, l + 1)]
        h, cast_outs = _dual_mm(xg, rs, gate_b, up_b, 0, 0, D_FF, "swiglu", BF16,
                                casts=[cast(ffn2_w_down, l)] + next_casts)
        if last:
            x, = _res_mm([h], cast_outs[0], x, FFN_RESIDUAL_SCALE)
        else:
            gate_b, up_b = cast_outs[1:]
            x, xg, rs = _res_mm([h], cast_outs[0], x, FFN_RESIDUAL_SCALE,
                                next_norm=(ffn1_norm, l + 1))
    out = _rmsnorm(x, final_norm.reshape(1, 1, d), 0, F32)
    return out.reshape(b, s_len, d)
```

```python
import functools
from typing import Any, Callable, NamedTuple

import jax
import jax.numpy as jnp
from jax import lax
from jax.experimental import pallas as pl
from jax.experimental.pallas import tpu as pltpu

D_MODEL = 4096
DEPTH = 4
N_HEADS = 16
HEAD_DIM = 128
ATTN_WIDTH = N_HEADS * HEAD_DIM
CONV_CHANNELS = D_MODEL - ATTN_WIDTH
IN_WIDTH = 3 * ATTN_WIDTH + 2 * CONV_CHANNELS
CONV_WIDTH = 31
D_FF = (3 * D_MODEL) // 2
ROPE_THETA = 500000.0
ROT_DIM = HEAD_DIM // 4
DILATED_BRANCHES = ((128, 1), (512, 4), (2048, 16))
BLOCK = 128
RMS_EPS = 1e-5
LN_EPS = 1e-5
FFN_RESIDUAL_SCALE = 0.5
MASK_VALUE = -1e30

LANES = 128
SUBLANES = 8
MXU_COLS = 256
CONV_HALO = 32
ATTN_ROWS = BLOCK * max(d for _, d in DILATED_BRANCHES)
MIB = 1024 * 1024

F32 = jnp.float32
BF16 = jnp.bfloat16


def _params(semantics, vmem_mib):
    return pltpu.CompilerParams(dimension_semantics=semantics,
                                vmem_limit_bytes=vmem_mib * MIB)


def _rmsnorm_kernel(x_ref, g_ref, o_ref):
    x = x_ref[...]
    ms = jnp.mean(x * x, axis=-1, keepdims=True)
    o_ref[...] = (x * lax.rsqrt(ms + RMS_EPS) * g_ref[...]).astype(o_ref.dtype)


def _rmsnorm(x, g, layer, out_dtype, tr=256):
    m, d = x.shape
    return pl.pallas_call(
        _rmsnorm_kernel,
        grid=(m // tr,),
        in_specs=[pl.BlockSpec((tr, d), lambda i: (i, 0)),
                  pl.BlockSpec((None, 1, d), lambda i: (layer, 0, 0))],
        out_specs=pl.BlockSpec((tr, d), lambda i: (i, 0)),
        out_shape=jax.ShapeDtypeStruct((m, d), out_dtype),
        compiler_params=_params(("parallel",), 32),
        name="rmsnorm",
    )(x, g)


def _prenorm_kernel(x_ref, g_ref, xg_ref, rs_ref):
    x = x_ref[...]
    ms = jnp.mean(x * x, axis=-1, keepdims=True)
    xg_ref[...] = (x * g_ref[...]).astype(xg_ref.dtype)
    rs_ref[...] = jnp.broadcast_to(lax.rsqrt(ms + RMS_EPS), rs_ref.shape)


def _prenorm(x, g, layer, tr=256):
    m, d = x.shape
    return pl.pallas_call(
        _prenorm_kernel,
        grid=(m // tr,),
        in_specs=[pl.BlockSpec((tr, d), lambda i: (i, 0)),
                  pl.BlockSpec((None, 1, d), lambda i: (layer, 0, 0))],
        out_specs=[pl.BlockSpec((tr, d), lambda i: (i, 0)),
                   pl.BlockSpec((tr, LANES), lambda i: (i, 0))],
        out_shape=[jax.ShapeDtypeStruct((m, d), BF16), jax.ShapeDtypeStruct((m, LANES), F32)],
        compiler_params=_params(("parallel",), 32),
        name="prenorm",
    )(x, g)


def _cast_plan(w, layer, block, n_steps_j):
    _, rows, cols = w.shape
    br, bc = block
    nbc = cols // bc
    n_blocks = (rows // br) * nbc

    def block_index(i, j):
        s = jnp.minimum(i * n_steps_j + j, n_blocks - 1)
        return s // nbc, s % nbc

    in_spec = pl.BlockSpec((None, br, bc), lambda i, j: (layer,) + block_index(i, j))
    out_spec = pl.BlockSpec((br, bc), block_index)
    return in_spec, out_spec, jax.ShapeDtypeStruct((rows, cols), BF16), n_blocks


def _run_casts(src_refs, dst_refs, n_blocks_list, n_steps):
    step = pl.program_id(0) * pl.num_programs(1) + pl.program_id(1)
    for src, dst, n_blocks in zip(src_refs, dst_refs, n_blocks_list):
        if n_blocks == n_steps:
            dst[...] = src[...].astype(dst.dtype)
        else:
            @pl.when(step < n_blocks)
            def _(src=src, dst=dst):
                dst[...] = src[...].astype(dst.dtype)


def _host_call(body, grid, in_specs, out_specs, out_shapes, operands, casts, vmem_mib, name,
               scratch_shapes=()):
    n_in, n_out, n_cast = len(in_specs), len(out_specs), len(casts)
    plans = [_cast_plan(w, layer, block, grid[1]) for w, layer, block in casts]
    n_steps = grid[0] * grid[1]
    assert all(p[3] <= n_steps for p in plans)
    n_blocks_list = [p[3] for p in plans]

    def kernel_fn(*refs):
        ins, cast_ins = refs[:n_in], refs[n_in:n_in + n_cast]
        outs = refs[n_in + n_cast:n_in + n_cast + n_out]
        cast_outs = refs[n_in + n_cast + n_out:n_in + 2 * n_cast + n_out]
        scratch = refs[n_in + 2 * n_cast + n_out:]
        _run_casts(cast_ins, cast_outs, n_blocks_list, n_steps)
        body(ins, outs, scratch)

    results = pl.pallas_call(
        kernel_fn,
        grid=grid,
        in_specs=list(in_specs) + [p[0] for p in plans],
        out_specs=list(out_specs) + [p[1] for p in plans],
        out_shape=list(out_shapes) + [p[2] for p in plans],
        scratch_shapes=list(scratch_shapes),
        compiler_params=_params(("arbitrary", "arbitrary"), vmem_mib),
        name=name,
    )(*operands, *[w for w, _, _ in casts])
    return results[:n_out], results[n_out:]


def _lane_tiles(width):
    return [slice(c * LANES, (c + 1) * LANES) for c in range(width // LANES)]


def _swiglu_mm(a, rs, w1, w2, casts=(), tm=1024, tn=512):
    m, k = a.shape
    n_out = w1.shape[1]

    def body(ins, outs, scratch):
        a_ref, rs_ref, w1_ref, w2_ref = ins
        o_ref, = outs
        x = a_ref[...]
        row_scale = rs_ref[...]
        for c0 in range(0, tn, MXU_COLS):
            chunk = slice(c0, c0 + MXU_COLS)
            p1 = jnp.dot(x, w1_ref[:, chunk], preferred_element_type=F32)
            p2 = jnp.dot(x, w2_ref[:, chunk], preferred_element_type=F32)
            for cols in _lane_tiles(MXU_COLS):
                g1, g2 = p1[:, cols] * row_scale, p2[:, cols] * row_scale
                o = (g1 * jax.nn.sigmoid(g1)) * g2
                o_ref[:, c0 + cols.start:c0 + cols.stop] = o.astype(o_ref.dtype)

    (out,), cast_outs = _host_call(
        body, (m // tm, n_out // tn),
        [pl.BlockSpec((tm, k), lambda i, j: (i, 0)),
         pl.BlockSpec((tm, LANES), lambda i, j: (i, 0)),
         pl.BlockSpec((k, tn), lambda i, j: (0, j)),
         pl.BlockSpec((k, tn), lambda i, j: (0, j))],
        [pl.BlockSpec((tm, tn), lambda i, j: (i, j))],
        [jax.ShapeDtypeStruct((m, n_out), BF16)],
        (a, rs, w1, w2), list(casts), 60, "swiglu_mm")
    return out, cast_outs


def _res_mm(a_list, w, res, scale, next_norm=None, tm=1024, tn=512):
    n_a = len(a_list)
    m, k = a_list[0].shape
    n = w.shape[-1]
    n_j = n // tn

    def body(ins, outs, scratch):
        a_refs, w_refs, res_ref = ins[:n_a], ins[n_a:2 * n_a], ins[2 * n_a]
        operands = [a_ref[...] for a_ref in a_refs]
        part = None
        for c0 in range(0, tn, MXU_COLS):
            chunk = slice(c0, c0 + MXU_COLS)
            acc = None
            for a, w_ref in zip(operands, w_refs):
                d = jnp.dot(a, w_ref[:, chunk], preferred_element_type=F32)
                acc = d if acc is None else acc + d
            if scale != 1.0:
                acc = scale * acc
            x = res_ref[:, chunk] + acc
            outs[0][:, chunk] = x
            if next_norm is None:
                continue
            outs[1][:, chunk] = (x * ins[2 * n_a + 1][:, chunk]).astype(outs[1].dtype)
            for cols in _lane_tiles(MXU_COLS):
                sq = x[:, cols] * x[:, cols]
                part = sq if part is None else part + sq
        if next_norm is None:
            return
        rs_ref = outs[2]
        ssq_ref, = scratch
        j = pl.program_id(1)

        @pl.when(j == 0)
        def _():
            ssq_ref[...] = part

        @pl.when(j > 0)
        def _():
            ssq_ref[...] += part

        @pl.when(j == n_j - 1)
        def _():
            ms = jnp.sum(ssq_ref[...], axis=-1, keepdims=True) * (1.0 / n)
            rs_ref[...] = jnp.broadcast_to(lax.rsqrt(ms + RMS_EPS), rs_ref.shape)

    in_specs = ([pl.BlockSpec((tm, k), lambda i, j: (i, 0)) for _ in a_list]
                + [pl.BlockSpec((k, tn), functools.partial(lambda i, j, s: (s, j), s=s))
                   for s in range(n_a)]
                + [pl.BlockSpec((tm, tn), lambda i, j: (i, j))])
    out_specs = [pl.BlockSpec((tm, tn), lambda i, j: (i, j))]
    out_shapes = [jax.ShapeDtypeStruct((m, n), F32)]
    operands = list(a_list) + [w] * n_a + [res]
    scratch_shapes = []
    if next_norm is not None:
        g, layer = next_norm
        in_specs.append(pl.BlockSpec((None, 1, tn), lambda i, j: (layer, 0, j)))
        operands.append(g)
        out_specs += [pl.BlockSpec((tm, tn), lambda i, j: (i, j)),
                      pl.BlockSpec((tm, LANES), lambda i, j: (i, 0))]
        out_shapes += [jax.ShapeDtypeStruct((m, n), BF16), jax.ShapeDtypeStruct((m, LANES), F32)]
        scratch_shapes.append(pltpu.VMEM((tm, LANES), F32))
    outs, _ = _host_call(body, (m // tm, n_j), in_specs, out_specs, out_shapes, operands, [],
                         60, "res_mm", scratch_shapes)
    return outs


def _qkv_mm(a, rs, w, rot, casts=(), tm=1024, tn=512):
    m, k = a.shape
    n_out = 3 * ATTN_WIDTH
    n_rot_tiles = 2 * ATTN_WIDTH // tn
    tab_spec = pl.BlockSpec((tm, HEAD_DIM), lambda i, j: (i, 0))

    def body(ins, outs, scratch):
        a_ref, rs_ref, w_ref, cos_ref, sin_lo_ref, sin_hi_ref = ins
        o_ref, = outs
        j = pl.program_id(1)

        def project(rotate):
            a, row_scale = a_ref[...], rs_ref[...]
            for c0 in range(0, tn, MXU_COLS):
                p = jnp.dot(a, w_ref[:, c0:c0 + MXU_COLS], preferred_element_type=F32)
                for cols in _lane_tiles(MXU_COLS):
                    x = p[:, cols] * row_scale
                    if rotate:
                        x = (x * cos_ref[...]
                             + pltpu.roll(x, HEAD_DIM - ROT_DIM // 2, 1) * sin_lo_ref[...]
                             + pltpu.roll(x, ROT_DIM // 2, 1) * sin_hi_ref[...])
                    o_ref[:, c0 + cols.start:c0 + cols.stop] = x

        pl.when(j < n_rot_tiles)(lambda: project(True))
        pl.when(j >= n_rot_tiles)(lambda: project(False))

    (out,), cast_outs = _host_call(
        body, (m // tm, n_out // tn),
        [pl.BlockSpec((tm, k), lambda i, j: (i, 0)),
         pl.BlockSpec((tm, LANES), lambda i, j: (i, 0)),
         pl.BlockSpec((k, tn), lambda i, j: (0, j)),
         tab_spec, tab_spec, tab_spec],
        [pl.BlockSpec((tm, tn), lambda i, j: (i, j))],
        [jax.ShapeDtypeStruct((m, n_out), F32)],
        (a, rs, w) + tuple(rot), list(casts), 60, "qkv_mm")
    return out, cast_outs


class _Tile(NamedTuple):
    sp_rows: Any
    q: Callable
    k: Callable
    v: Callable
    dst_rows: Any


class _TileKind(NamedTuple):
    n: int
    branch: int
    mask: Any
    tile: Callable


def _aligned(x, multiple):
    return x if isinstance(x, int) else pl.multiple_of(x, multiple)


def _for_groups(n, max_group, fn):
    group = max(g for g in range(1, max_group + 1) if n % g == 0)
    if n == group:
        fn(list(range(group)))
        return

    def body(it, carry):
        fn([it * group + g for g in range(group)])
        return carry

    lax.fori_loop(0, n // group, body, 0)


def _attn_kernel(q_ref, kc_ref, vc_ref, out_ref, qt_ref, kt_ref, vt_ref, pkt_ref, pvt_ref,
                 pk_tail_ref, pv_tail_ref, s_ref, p_ref, *merge_bufs, branches, max_group,
                 merge_chunk):
    (w1, d1), (w2, d2), (w3, d3) = branches
    rows_total, width = q_ref.shape
    assert width == HEAD_DIM and d1 == 1 and d3 % d2 == 0 and rows_total == BLOCK * d3
    e = d3 // d2
    res_rows = rows_total // d2
    nb1 = rows_total // BLOCK
    nb2 = res_rows // BLOCK
    nums, maxs, dens = merge_bufs[0::3], merge_bufs[1::3], merge_bufs[2::3]
    step = pl.program_id(1)

    @pl.when(step == 0)
    def _():
        for ref in (pkt_ref, pvt_ref, pk_tail_ref, pv_tail_ref):
            ref[...] = jnp.zeros(ref.shape, ref.dtype)

    for r2 in range(d2):
        src = pl.ds(r2, res_rows, stride=d2)
        dst = pl.ds(r2 * res_rows, res_rows)
        qt_ref[dst, :] = q_ref[src, :]
        kt_ref[dst, :] = kc_ref[src, :]
        vt_ref[dst, :] = vc_ref[src, :]

    qi = lax.broadcasted_iota(jnp.int32, (BLOCK, 2 * BLOCK), 0) + BLOCK
    kj = lax.broadcasted_iota(jnp.int32, (BLOCK, 2 * BLOCK), 1)
    delta = qi - kj
    has_prev = (kj >= BLOCK) | (step > 0)

    def band(window, d):
        return (delta >= 0) & (delta <= window // d)

    def sp_rows(t):
        return pl.ds(_aligned(t * BLOCK, BLOCK), BLOCK)

    def cat(prev_ref, prev_rows, cur_ref, cur_rows):
        return lambda: jnp.concatenate([prev_ref[prev_rows, :], cur_ref[cur_rows, :]], axis=0)

    def b1_first(idx):
        rows = pl.ds(0, BLOCK)
        return _Tile(sp_rows(0), lambda: q_ref[rows, :],
                     cat(pk_tail_ref, slice(None), kc_ref, rows),
                     cat(pv_tail_ref, slice(None), vc_ref, rows), rows)

    def b1_later(idx):
        base = _aligned((idx + 1) * BLOCK, BLOCK)
        rows, kv_rows = pl.ds(base, BLOCK), pl.ds(base - BLOCK, 2 * BLOCK)
        return _Tile(sp_rows(idx + 1), lambda: q_ref[rows, :],
                     lambda: kc_ref[kv_rows, :], lambda: vc_ref[kv_rows, :], rows)

    def b2_first(r2):
        base = _aligned(r2 * res_rows, BLOCK)
        rows, prev_rows = pl.ds(base, BLOCK), pl.ds(base + res_rows - BLOCK, BLOCK)
        return _Tile(sp_rows(nb1 + r2 * nb2), lambda: qt_ref[rows, :],
                     cat(pkt_ref, prev_rows, kt_ref, rows),
                     cat(pvt_ref, prev_rows, vt_ref, rows), rows)

    def b2_later(idx):
        r2, b = idx % d2, idx // d2 + 1
        base = _aligned(r2 * res_rows + b * BLOCK, BLOCK)
        rows, kv_rows = pl.ds(base, BLOCK), pl.ds(base - BLOCK, 2 * BLOCK)
        return _Tile(sp_rows(nb1 + r2 * nb2 + b), lambda: qt_ref[rows, :],
                     lambda: kt_ref[kv_rows, :], lambda: vt_ref[kv_rows, :], rows)

    def b3_all(idx):
        r2, off = idx % d2, idx // d2
        rows = pl.ds(r2 * res_rows + off, BLOCK, stride=e)
        return _Tile(sp_rows(nb1 + d2 * nb2 + idx), lambda: qt_ref[rows, :],
                     cat(pkt_ref, rows, kt_ref, rows), cat(pvt_ref, rows, vt_ref, rows), rows)

    kinds = [
        _TileKind(1, 0, band(w1, d1) & has_prev, b1_first),
        _TileKind(nb1 - 1, 0, band(w1, d1), b1_later),
        _TileKind(d2, 1, band(w2, d2) & has_prev, b2_first),
        _TileKind(d2 * (nb2 - 1), 1, band(w2, d2), b2_later),
        _TileKind(d3, 2, band(w3, d3) & has_prev, b3_all),
    ]
    scale = HEAD_DIM ** -0.5

    for kind in kinds:
        def scores(idxs, kind=kind):
            for t in [kind.tile(i) for i in idxs]:
                s = lax.dot_general(t.q().astype(BF16), t.k().astype(BF16),
                                    (((1,), (1,)), ((), ())), preferred_element_type=F32)
                s_ref[t.sp_rows, :] = jnp.where(kind.mask, s * scale, MASK_VALUE)
        _for_groups(kind.n, max_group, scores)

    for kind in kinds:
        def softmax(idxs, kind=kind):
            for t in [kind.tile(i) for i in idxs]:
                s = s_ref[t.sp_rows, :]
                m = jnp.max(s, axis=-1, keepdims=True)
                p_ref[t.sp_rows, :] = jnp.exp(s - m).astype(p_ref.dtype)
                maxs[kind.branch][t.dst_rows, :] = jnp.broadcast_to(m, (BLOCK, HEAD_DIM))
        _for_groups(kind.n, max_group, softmax)

    ones = jnp.ones((2 * BLOCK, HEAD_DIM), BF16)
    for kind in kinds:
        def values(idxs, kind=kind):
            for t in [kind.tile(i) for i in idxs]:
                v1 = jnp.concatenate([t.v().astype(BF16), ones], axis=1)
                o = jnp.dot(p_ref[t.sp_rows, :], v1, preferred_element_type=F32)
                nums[kind.branch][t.dst_rows, :] = o[:, :HEAD_DIM]
                dens[kind.branch][t.dst_rows, :] = o[:, HEAD_DIM:]
        _for_groups(kind.n, max_group, values)

    def merge(it, carry):
        for r2 in range(d2):
            res = pl.ds(pl.multiple_of(r2 * res_rows + it * merge_chunk, merge_chunk), merge_chunk)
            nat = pl.ds(r2 + it * merge_chunk * d2, merge_chunk, stride=d2)
            rows = (nat, res, res)
            ms = [maxs[b][rows[b], :] for b in range(3)]
            top = jnp.maximum(jnp.maximum(ms[0], ms[1]), ms[2])
            ws = [jnp.exp(m - top) for m in ms]
            numer = ws[0] * nums[0][rows[0], :]
            denom = ws[0] * dens[0][rows[0], :]
            for b in (1, 2):
                numer = numer + ws[b] * nums[b][rows[b], :]
                denom = denom + ws[b] * dens[b][rows[b], :]
            out_ref[nat, :] = numer * (1.0 / denom)
        return carry

    lax.fori_loop(0, res_rows // merge_chunk, merge, 0)

    pkt_ref[...] = kt_ref[...]
    pvt_ref[...] = vt_ref[...]
    pk_tail_ref[...] = kc_ref[pl.ds(rows_total - BLOCK, BLOCK), :]
    pv_tail_ref[...] = vc_ref[pl.ds(rows_total - BLOCK, BLOCK), :]


def _attention(qkv):
    s_len = qkv.shape[0]
    assert s_len % ATTN_ROWS == 0 and len(DILATED_BRANCHES) == 3
    n_tiles = sum(ATTN_ROWS // BLOCK for _ in DILATED_BRANCHES)
    blk = (ATTN_ROWS, HEAD_DIM)

    def spec(part):
        return pl.BlockSpec(blk, lambda h, m: (m, part * N_HEADS + h))

    return pl.pallas_call(
        functools.partial(_attn_kernel, branches=DILATED_BRANCHES, max_group=16, merge_chunk=32),
        grid=(N_HEADS, s_len // ATTN_ROWS),
        in_specs=[spec(0), spec(1), spec(2)],
        out_specs=pl.BlockSpec(blk, lambda h, m: (m, h)),
        out_shape=jax.ShapeDtypeStruct((s_len, ATTN_WIDTH), F32),
        scratch_shapes=([pltpu.VMEM(blk, F32) for _ in range(5)]
                        + [pltpu.VMEM((BLOCK, HEAD_DIM), F32) for _ in range(2)]
                        + [pltpu.VMEM((n_tiles * BLOCK, 2 * BLOCK), F32),
                           pltpu.VMEM((n_tiles * BLOCK, 2 * BLOCK), BF16)]
                        + [pltpu.VMEM(blk, F32) for _ in range(9)]),
        compiler_params=_params(("arbitrary", "arbitrary"), 48),
        name="attention",
    )(qkv, qkv, qkv)


def _glu_conv_kernel(a_ref, rs_ref, wa_ref, wg_ref, cw_ref, cb_ref, out_ref,
                     hbuf_ref, halo_ref, shift_ref, *, row_chunk):
    i, j = pl.program_id(0), pl.program_id(1)
    tm, tn = out_ref.shape
    tile_cols = pl.ds(pl.multiple_of(j * tn, tn), tn)

    @pl.when(i == 0)
    def _():
        hbuf_ref[0:CONV_HALO, :] = jnp.zeros((CONV_HALO, tn), F32)

    @pl.when(i > 0)
    def _():
        hbuf_ref[0:CONV_HALO, :] = halo_ref[:, tile_cols]

    x, row_scale = a_ref[...], rs_ref[...]
    first = CONV_HALO - (CONV_WIDTH - 1)
    ext = shift_ref.shape[2]
    for c0 in range(0, tn, MXU_COLS):
        chunk = slice(c0, c0 + MXU_COLS)
        pa = jnp.dot(x, wa_ref[:, chunk], preferred_element_type=F32)
        pg = jnp.dot(x, wg_ref[:, chunk], preferred_element_type=F32)
        for t, cols in enumerate(_lane_tiles(MXU_COLS)):
            lanes = slice(c0 + cols.start, c0 + cols.stop)
            hbuf_ref[CONV_HALO:, lanes] = ((pa[:, cols] * row_scale)
                                           * jax.nn.sigmoid(pg[:, cols] * row_scale))
            for s in range(1, SUBLANES):
                shift_ref[t, s - 1, :, :] = hbuf_ref[pl.ds(s, ext), lanes]
            w = cw_ref[:, lanes]
            b = cb_ref[:, lanes]
            for base in range(0, tm, row_chunk):
                acc = None
                for tap in range(CONV_WIDTH):
                    off = first + tap
                    s, aligned = off % SUBLANES, off - off % SUBLANES
                    if s == 0:
                        rows = hbuf_ref[pl.ds(base + aligned, row_chunk), lanes]
                    else:
                        rows = shift_ref[t, s - 1, pl.ds(base + aligned, row_chunk), :]
                    term = w[tap:tap + 1, :] * rows
                    acc = term if acc is None else acc + term
                out_ref[pl.ds(base, row_chunk), lanes] = acc + b
    halo_ref[:, tile_cols] = hbuf_ref[tm:tm + CONV_HALO, :]


def _glu_conv(a, rs, w, off_a, off_g, conv_w, conv_b, layer, tm=512, tn=1024):
    m, k = a.shape
    c = CONV_CHANNELS
    oa, og = off_a // tn, off_g // tn
    return pl.pallas_call(
        functools.partial(_glu_conv_kernel, row_chunk=64),
        grid=(m // tm, c // tn),
        in_specs=[pl.BlockSpec((tm, k), lambda i, j: (i, 0)),
                  pl.BlockSpec((tm, LANES), lambda i, j: (i, 0)),
                  pl.BlockSpec((k, tn), lambda i, j: (0, oa + j)),
                  pl.BlockSpec((k, tn), lambda i, j: (0, og + j)),
                  pl.BlockSpec((None, CONV_WIDTH, tn), lambda i, j: (layer, 0, j)),
                  pl.BlockSpec((None, 1, tn), lambda i, j: (layer, 0, j))],
        out_specs=pl.BlockSpec((tm, tn), lambda i, j: (i, j)),
        out_shape=jax.ShapeDtypeStruct((m, c), F32),
        scratch_shapes=[pltpu.VMEM((tm + CONV_HALO, tn), F32),
                        pltpu.VMEM((CONV_HALO, c), F32),
                        pltpu.VMEM((MXU_COLS // LANES, SUBLANES - 1, tm + CONV_HALO - SUBLANES, LANES),
                                   F32)],
        compiler_params=_params(("arbitrary", "arbitrary"), 56),
        name="glu_conv",
    )(a, rs, w, w, conv_w, conv_b)


def _conv_norm_kernel(x_ref, lng_ref, lnb_ref, g_ref, out_ref, *, ln_chunk):
    def norm_rows(rc, carry):
        rows = pl.ds(pl.multiple_of(rc * ln_chunk, ln_chunk), ln_chunk)
        x = x_ref[rows, :]
        mu = jnp.mean(x, axis=-1, keepdims=True)
        xc = x - mu
        var = jnp.mean(xc * xc, axis=-1, keepdims=True)
        y = xc * lax.rsqrt(var + LN_EPS) * lng_ref[...] + lnb_ref[...]
        y = y * jax.nn.sigmoid(y)
        ms = jnp.mean(y * y, axis=-1, keepdims=True)
        out_ref[rows, :] = (y * lax.rsqrt(ms + RMS_EPS) * g_ref[...]).astype(out_ref.dtype)
        return carry

    lax.fori_loop(0, x_ref.shape[0] // ln_chunk, norm_rows, 0, unroll=4)


def _conv_norm(x, ln_g, ln_b, out_g, layer, ts=256):
    s_len, c = x.shape
    vec_spec = pl.BlockSpec((None, 1, c), lambda i: (layer, 0, 0))
    return pl.pallas_call(
        functools.partial(_conv_norm_kernel, ln_chunk=16),
        grid=(s_len // ts,),
        in_specs=[pl.BlockSpec((ts, c), lambda i: (i, 0)), vec_spec, vec_spec, vec_spec],
        out_specs=pl.BlockSpec((ts, c), lambda i: (i, 0)),
        out_shape=jax.ShapeDtypeStruct((s_len, c), BF16),
        compiler_params=_params(("parallel",), 32),
        name="conv_norm",
    )(x, ln_g, ln_b, out_g)


def _rotary_tables(seq):
    pos = jnp.arange(seq, dtype=F32)
    inv_freq = ROPE_THETA ** (-(jnp.arange(0, ROT_DIM, 2, dtype=F32) / ROT_DIM))
    ang = pos[:, None] * inv_freq[None, :]
    cos, sin = jnp.cos(ang), jnp.sin(ang)
    half = ROT_DIM // 2
    rest = HEAD_DIM - ROT_DIM
    ones = jnp.ones((seq, rest), F32)
    zeros_h = jnp.zeros((seq, half), F32)
    zeros_r = jnp.zeros((seq, rest), F32)
    cos_t = jnp.concatenate([cos, cos, ones], axis=-1)
    sin_lo = jnp.concatenate([-sin, zeros_h, zeros_r], axis=-1)
    sin_hi = jnp.concatenate([zeros_h, sin, zeros_r], axis=-1)
    return cos_t, sin_lo, sin_hi


def _cast_block(w, n_steps):
    _, rows, cols = w.shape
    for br, bc in ((512, 512), (512, 1024), (1024, 1024)):
        if rows % br == 0 and cols % bc == 0 and (rows // br) * (cols // bc) <= n_steps:
            return br, bc
    raise ValueError("no cast block for %s in %d steps" % (w.shape, n_steps))


def kernel(x, ffn1_norm, ffn1_w_gate, ffn1_w_up, ffn1_w_down, mix_norm, w_in, conv_w, conv_b, conv_ln_g, conv_ln_b, attn_out_norm, conv_out_norm, w_out, ffn2_norm, ffn2_w_gate, ffn2_w_up, ffn2_w_down, final_norm):
    b, s_len, d = x.shape
    assert b == 1 and d == D_MODEL
    x = x.reshape(s_len, d)
    rot = _rotary_tables(s_len)
    a, c = ATTN_WIDTH, CONV_CHANNELS

    def vec(p):
        return p.reshape(p.shape[0], 1, p.shape[1])

    ffn1_norm, mix_norm, ffn2_norm = vec(ffn1_norm), vec(mix_norm), vec(ffn2_norm)
    conv_b, conv_ln_g, conv_ln_b = vec(conv_b), vec(conv_ln_g), vec(conv_ln_b)
    attn_out_norm, conv_out_norm = vec(attn_out_norm), vec(conv_out_norm)
    host_steps = (s_len // 1024) * (D_FF // 512)

    def cast(w, layer):
        return w, layer, _cast_block(w, host_steps)

    gate_b, up_b = ffn1_w_gate[0].astype(BF16), ffn1_w_up[0].astype(BF16)
    xg, rs = _prenorm(x, ffn1_norm, 0)
    for l in range(DEPTH):
        last = l + 1 == DEPTH
        h, (down_b, w_in_b, w_out_b) = _swiglu_mm(
            xg, rs, gate_b, up_b,
            casts=[cast(ffn1_w_down, l), cast(w_in, l), cast(w_out, l)])
        x, xg, rs = _res_mm([h], down_b, x, FFN_RESIDUAL_SCALE, next_norm=(mix_norm, l))
        qkv, (gate_b, up_b) = _qkv_mm(xg, rs, w_in_b, rot,
                                      casts=[cast(ffn2_w_gate, l), cast(ffn2_w_up, l)])
        conv = _glu_conv(xg, rs, w_in_b, 3 * a, 3 * a + c, conv_w, conv_b, l)
        attn_n = _rmsnorm(_attention(qkv), attn_out_norm, l, BF16)
        conv_n = _conv_norm(conv, conv_ln_g, conv_ln_b, conv_out_norm, l)
        x, xg, rs = _res_mm([attn_n, conv_n], w_out_b, x, 1.0, next_norm=(ffn2_norm, l))
        next_casts = [] if last else [cast(ffn1_w_gate, l + 1), cast(ffn1_w_up, l + 1)]
        h, cast_outs = _swiglu_mm(xg, rs, gate_b, up_b,
                                  casts=[cast(ffn2_w_down, l)] + next_casts)
        if last:
            x, = _res_mm([h], cast_outs[0], x, FFN_RESIDUAL_SCALE)
        else:
            gate_b, up_b = cast_outs[1:]
            x, xg, rs = _res_mm([h], cast_outs[0], x, FFN_RESIDUAL_SCALE,
                                next_norm=(ffn1_norm, l + 1))
    out = _rmsnorm(x, final_norm.reshape(1, 1, d), 0, F32)
    return out.reshape(b, s_len, d)
```

```python
import functools
from typing import Any, Callable, NamedTuple

import jax
import jax.numpy as jnp
from jax import lax
from jax.experimental import pallas as pl
from jax.experimental.pallas import tpu as pltpu

D_MODEL = 4096
DEPTH = 4
N_HEADS = 16
HEAD_DIM = 128
ATTN_WIDTH = N_HEADS * HEAD_DIM
CONV_CHANNELS = D_MODEL - ATTN_WIDTH
IN_WIDTH = 3 * ATTN_WIDTH + 2 * CONV_CHANNELS
CONV_WIDTH = 31
D_FF = (3 * D_MODEL) // 2
ROPE_THETA = 500000.0
ROT_DIM = HEAD_DIM // 4
DILATED_BRANCHES = ((128, 1), (512, 4), (2048, 16))
BLOCK = 128
RMS_EPS = 1e-5
LN_EPS = 1e-5
FFN_RESIDUAL_SCALE = 0.5
MASK_VALUE = -1e30

LANES = 128
SUBLANES = 8
MXU_COLS = 256
CONV_HALO = 32
ATTN_ROWS = BLOCK * max(d for _, d in DILATED_BRANCHES)
MIB = 1024 * 1024
NORM_BLOCK_BYTES = 4 * MIB

F32 = jnp.float32
BF16 = jnp.bfloat16


def _params(semantics, vmem_mib):
    return pltpu.CompilerParams(dimension_semantics=semantics,
                                vmem_limit_bytes=vmem_mib * MIB)


def _rmsnorm_kernel(x_ref, g_ref, o_ref):
    x = x_ref[...]
    ms = jnp.mean(x * x, axis=-1, keepdims=True)
    o_ref[...] = (x * lax.rsqrt(ms + RMS_EPS) * g_ref[...]).astype(o_ref.dtype)


def _rmsnorm(x, g, layer, out_dtype):
    m, d = x.shape
    tr = NORM_BLOCK_BYTES // (4 * d)
    return pl.pallas_call(
        _rmsnorm_kernel,
        grid=(m // tr,),
        in_specs=[pl.BlockSpec((tr, d), lambda i: (i, 0)),
                  pl.BlockSpec((None, 1, d), lambda i: (layer, 0, 0))],
        out_specs=pl.BlockSpec((tr, d), lambda i: (i, 0)),
        out_shape=jax.ShapeDtypeStruct((m, d), out_dtype),
        compiler_params=_params(("parallel",), 32),
        name="rmsnorm",
    )(x, g)


def _prenorm_kernel(x_ref, g_ref, xg_ref, rs_ref):
    x = x_ref[...]
    ms = jnp.mean(x * x, axis=-1, keepdims=True)
    xg_ref[...] = (x * g_ref[...]).astype(xg_ref.dtype)
    rs_ref[...] = jnp.broadcast_to(lax.rsqrt(ms + RMS_EPS), rs_ref.shape)


def _prenorm(x, g, layer, tr=256):
    m, d = x.shape
    return pl.pallas_call(
        _prenorm_kernel,
        grid=(m // tr,),
        in_specs=[pl.BlockSpec((tr, d), lambda i: (i, 0)),
                  pl.BlockSpec((None, 1, d), lambda i: (layer, 0, 0))],
        out_specs=[pl.BlockSpec((tr, d), lambda i: (i, 0)),
                   pl.BlockSpec((tr, LANES), lambda i: (i, 0))],
        out_shape=[jax.ShapeDtypeStruct((m, d), BF16), jax.ShapeDtypeStruct((m, LANES), F32)],
        compiler_params=_params(("parallel",), 32),
        name="prenorm",
    )(x, g)


def _cast_plan(w, layer, block, n_steps_j):
    _, rows, cols = w.shape
    br, bc = block
    nbc = cols // bc
    n_blocks = (rows // br) * nbc

    def block_index(i, j):
        s = jnp.minimum(i * n_steps_j + j, n_blocks - 1)
        return s // nbc, s % nbc

    in_spec = pl.BlockSpec((None, br, bc), lambda i, j: (layer,) + block_index(i, j))
    out_spec = pl.BlockSpec((br, bc), block_index)
    return in_spec, out_spec, jax.ShapeDtypeStruct((rows, cols), BF16), n_blocks


def _run_casts(src_refs, dst_refs, n_blocks_list, n_steps):
    step = pl.program_id(0) * pl.num_programs(1) + pl.program_id(1)
    for src, dst, n_blocks in zip(src_refs, dst_refs, n_blocks_list):
        if n_blocks == n_steps:
            dst[...] = src[...].astype(dst.dtype)
        else:
            @pl.when(step < n_blocks)
            def _(src=src, dst=dst):
                dst[...] = src[...].astype(dst.dtype)


def _host_call(body, grid, in_specs, out_specs, out_shapes, operands, casts, vmem_mib, name,
               scratch_shapes=()):
    n_in, n_out, n_cast = len(in_specs), len(out_specs), len(casts)
    plans = [_cast_plan(w, layer, block, grid[1]) for w, layer, block in casts]
    n_steps = grid[0] * grid[1]
    assert all(p[3] <= n_steps for p in plans)
    n_blocks_list = [p[3] for p in plans]

    def kernel_fn(*refs):
        ins, cast_ins = refs[:n_in], refs[n_in:n_in + n_cast]
        outs = refs[n_in + n_cast:n_in + n_cast + n_out]
        cast_outs = refs[n_in + n_cast + n_out:n_in + 2 * n_cast + n_out]
        scratch = refs[n_in + 2 * n_cast + n_out:]
        _run_casts(cast_ins, cast_outs, n_blocks_list, n_steps)
        body(ins, outs, scratch)

    results = pl.pallas_call(
        kernel_fn,
        grid=grid,
        in_specs=list(in_specs) + [p[0] for p in plans],
        out_specs=list(out_specs) + [p[1] for p in plans],
        out_shape=list(out_shapes) + [p[2] for p in plans],
        scratch_shapes=list(scratch_shapes),
        compiler_params=_params(("arbitrary", "arbitrary"), vmem_mib),
        name=name,
    )(*operands, *[w for w, _, _ in casts])
    return results[:n_out], results[n_out:]


def _lane_tiles(width):
    return [slice(c * LANES, (c + 1) * LANES) for c in range(width // LANES)]


def _swiglu_mm(a, rs, w1, w2, casts=(), tm=1024, tn=512):
    m, k = a.shape
    n_out = w1.shape[1]

    def body(ins, outs, scratch):
        a_ref, rs_ref, w1_ref, w2_ref = ins
        o_ref, = outs
        x = a_ref[...]
        row_scale = rs_ref[...]
        for c0 in range(0, tn, MXU_COLS):
            chunk = slice(c0, c0 + MXU_COLS)
            p1 = jnp.dot(x, w1_ref[:, chunk], preferred_element_type=F32)
            p2 = jnp.dot(x, w2_ref[:, chunk], preferred_element_type=F32)
            for cols in _lane_tiles(MXU_COLS):
                g1, g2 = p1[:, cols] * row_scale, p2[:, cols] * row_scale
                o = (g1 * jax.nn.sigmoid(g1)) * g2
                o_ref[:, c0 + cols.start:c0 + cols.stop] = o.astype(o_ref.dtype)

    (out,), cast_outs = _host_call(
        body, (m // tm, n_out // tn),
        [pl.BlockSpec((tm, k), lambda i, j: (i, 0)),
         pl.BlockSpec((tm, LANES), lambda i, j: (i, 0)),
         pl.BlockSpec((k, tn), lambda i, j: (0, j)),
         pl.BlockSpec((k, tn), lambda i, j: (0, j))],
        [pl.BlockSpec((tm, tn), lambda i, j: (i, j))],
        [jax.ShapeDtypeStruct((m, n_out), BF16)],
        (a, rs, w1, w2), list(casts), 60, "swiglu_mm")
    return out, cast_outs


def _res_mm(a_list, w, res, scale, next_norm=None, tm=1024, tn=512):
    n_a = len(a_list)
    m, k = a_list[0].shape
    n = w.shape[-1]
    n_j = n // tn

    def body(ins, outs, scratch):
        a_refs, w_refs, res_ref = ins[:n_a], ins[n_a:2 * n_a], ins[2 * n_a]
        operands = [a_ref[...] for a_ref in a_refs]
        part = None
        for c0 in range(0, tn, MXU_COLS):
            chunk = slice(c0, c0 + MXU_COLS)
            acc = None
            for a, w_ref in zip(operands, w_refs):
                d = jnp.dot(a, w_ref[:, chunk], preferred_element_type=F32)
                acc = d if acc is None else acc + d
            if scale != 1.0:
                acc = scale * acc
            x = res_ref[:, chunk] + acc
            outs[0][:, chunk] = x
            if next_norm is None:
                continue
            outs[1][:, chunk] = (x * ins[2 * n_a + 1][:, chunk]).astype(outs[1].dtype)
            for cols in _lane_tiles(MXU_COLS):
                sq = x[:, cols] * x[:, cols]
                part = sq if part is None else part + sq
        if next_norm is None:
            return
        rs_ref = outs[2]
        ssq_ref, = scratch
        j = pl.program_id(1)

        @pl.when(j == 0)
        def _():
            ssq_ref[...] = part

        @pl.when(j > 0)
        def _():
            ssq_ref[...] += part

        @pl.when(j == n_j - 1)
        def _():
            ms = jnp.sum(ssq_ref[...], axis=-1, keepdims=True) * (1.0 / n)
            rs_ref[...] = jnp.broadcast_to(lax.rsqrt(ms + RMS_EPS), rs_ref.shape)

    in_specs = ([pl.BlockSpec((tm, k), lambda i, j: (i, 0)) for _ in a_list]
                + [pl.BlockSpec((k, tn), functools.partial(lambda i, j, s: (s, j), s=s))
                   for s in range(n_a)]
                + [pl.BlockSpec((tm, tn), lambda i, j: (i, j))])
    out_specs = [pl.BlockSpec((tm, tn), lambda i, j: (i, j))]
    out_shapes = [jax.ShapeDtypeStruct((m, n), F32)]
    operands = list(a_list) + [w] * n_a + [res]
    scratch_shapes = []
    if next_norm is not None:
        g, layer = next_norm
        in_specs.append(pl.BlockSpec((None, 1, tn), lambda i, j: (layer, 0, j)))
        operands.append(g)
        out_specs += [pl.BlockSpec((tm, tn), lambda i, j: (i, j)),
                      pl.BlockSpec((tm, LANES), lambda i, j: (i, 0))]
        out_shapes += [jax.ShapeDtypeStruct((m, n), BF16), jax.ShapeDtypeStruct((m, LANES), F32)]
        scratch_shapes.append(pltpu.VMEM((tm, LANES), F32))
    outs, _ = _host_call(body, (m // tm, n_j), in_specs, out_specs, out_shapes, operands, [],
                         60, "res_mm", scratch_shapes)
    return outs


def _qkv_mm(a, rs, w, rot, casts=(), tm=1024, tn=512):
    m, k = a.shape
    n_out = 3 * ATTN_WIDTH
    n_rot_tiles = 2 * ATTN_WIDTH // tn
    tab_spec = pl.BlockSpec((tm, HEAD_DIM), lambda i, j: (i, 0))

    def body(ins, outs, scratch):
        a_ref, rs_ref, w_ref, cos_ref, sin_lo_ref, sin_hi_ref = ins
        o_ref, = outs
        j = pl.program_id(1)

        def project(rotate):
            a, row_scale = a_ref[...], rs_ref[...]
            for c0 in range(0, tn, MXU_COLS):
                p = jnp.dot(a, w_ref[:, c0:c0 + MXU_COLS], preferred_element_type=F32)
                for cols in _lane_tiles(MXU_COLS):
                    x = p[:, cols] * row_scale
                    if rotate:
                        x = (x * cos_ref[...]
                             + pltpu.roll(x, HEAD_DIM - ROT_DIM // 2, 1) * sin_lo_ref[...]
                             + pltpu.roll(x, ROT_DIM // 2, 1) * sin_hi_ref[...])
                    o_ref[:, c0 + cols.start:c0 + cols.stop] = x

        pl.when(j < n_rot_tiles)(lambda: project(True))
        pl.when(j >= n_rot_tiles)(lambda: project(False))

    (out,), cast_outs = _host_call(
        body, (m // tm, n_out // tn),
        [pl.BlockSpec((tm, k), lambda i, j: (i, 0)),
         pl.BlockSpec((tm, LANES), lambda i, j: (i, 0)),
         pl.BlockSpec((k, tn), lambda i, j: (0, j)),
         tab_spec, tab_spec, tab_spec],
        [pl.BlockSpec((tm, tn), lambda i, j: (i, j))],
        [jax.ShapeDtypeStruct((m, n_out), F32)],
        (a, rs, w) + tuple(rot), list(casts), 60, "qkv_mm")
    return out, cast_outs


class _Tile(NamedTuple):
    sp_rows: Any
    q: Callable
    k: Callable
    v: Callable
    dst_rows: Any


class _TileKind(NamedTuple):
    n: int
    branch: int
    mask: Any
    tile: Callable


def _aligned(x, multiple):
    return x if isinstance(x, int) else pl.multiple_of(x, multiple)


def _for_groups(n, max_group, fn):
    group = max(g for g in range(1, max_group + 1) if n % g == 0)
    if n == group:
        fn(list(range(group)))
        return

    def body(it, carry):
        fn([it * group + g for g in range(group)])
        return carry

    lax.fori_loop(0, n // group, body, 0)


def _attn_kernel(q_ref, kc_ref, vc_ref, out_ref, qt_ref, kt_ref, vt_ref, pkt_ref, pvt_ref,
                 pk_tail_ref, pv_tail_ref, s_ref, p_ref, *merge_bufs, branches, max_group,
                 merge_chunk):
    (w1, d1), (w2, d2), (w3, d3) = branches
    rows_total, width = q_ref.shape
    assert width == HEAD_DIM and d1 == 1 and d3 % d2 == 0 and rows_total == BLOCK * d3
    e = d3 // d2
    res_rows = rows_total // d2
    nb1 = rows_total // BLOCK
    nb2 = res_rows // BLOCK
    nums, maxs, dens = merge_bufs[0::3], merge_bufs[1::3], merge_bufs[2::3]
    step = pl.program_id(1)

    @pl.when(step == 0)
    def _():
        for ref in (pkt_ref, pvt_ref, pk_tail_ref, pv_tail_ref):
            ref[...] = jnp.zeros(ref.shape, ref.dtype)

    for r2 in range(d2):
        src = pl.ds(r2, res_rows, stride=d2)
        dst = pl.ds(r2 * res_rows, res_rows)
        qt_ref[dst, :] = q_ref[src, :]
        kt_ref[dst, :] = kc_ref[src, :]
        vt_ref[dst, :] = vc_ref[src, :]

    qi = lax.broadcasted_iota(jnp.int32, (BLOCK, 2 * BLOCK), 0) + BLOCK
    kj = lax.broadcasted_iota(jnp.int32, (BLOCK, 2 * BLOCK), 1)
    delta = qi - kj
    has_prev = (kj >= BLOCK) | (step > 0)

    def band(window, d):
        return (delta >= 0) & (delta <= window // d)

    def sp_rows(t):
        return pl.ds(_aligned(t * BLOCK, BLOCK), BLOCK)

    def cat(prev_ref, prev_rows, cur_ref, cur_rows):
        return lambda: jnp.concatenate([prev_ref[prev_rows, :], cur_ref[cur_rows, :]], axis=0)

    def b1_first(idx):
        rows = pl.ds(0, BLOCK)
        return _Tile(sp_rows(0), lambda: q_ref[rows, :],
                     cat(pk_tail_ref, slice(None), kc_ref, rows),
                     cat(pv_tail_ref, slice(None), vc_ref, rows), rows)

    def b1_later(idx):
        base = _aligned((idx + 1) * BLOCK, BLOCK)
        rows, kv_rows = pl.ds(base, BLOCK), pl.ds(base - BLOCK, 2 * BLOCK)
        return _Tile(sp_rows(idx + 1), lambda: q_ref[rows, :],
                     lambda: kc_ref[kv_rows, :], lambda: vc_ref[kv_rows, :], rows)

    def b2_first(r2):
        base = _aligned(r2 * res_rows, BLOCK)
        rows, prev_rows = pl.ds(base, BLOCK), pl.ds(base + res_rows - BLOCK, BLOCK)
        return _Tile(sp_rows(nb1 + r2 * nb2), lambda: qt_ref[rows, :],
                     cat(pkt_ref, prev_rows, kt_ref, rows),
                     cat(pvt_ref, prev_rows, vt_ref, rows), rows)

    def b2_later(idx):
        r2, b = idx % d2, idx // d2 + 1
        base = _aligned(r2 * res_rows + b * BLOCK, BLOCK)
        rows, kv_rows = pl.ds(base, BLOCK), pl.ds(base - BLOCK, 2 * BLOCK)
        return _Tile(sp_rows(nb1 + r2 * nb2 + b), lambda: qt_ref[rows, :],
                     lambda: kt_ref[kv_rows, :], lambda: vt_ref[kv_rows, :], rows)

    def b3_all(idx):
        r2, off = idx % d2, idx // d2
        rows = pl.ds(r2 * res_rows + off, BLOCK, stride=e)
        return _Tile(sp_rows(nb1 + d2 * nb2 + idx), lambda: qt_ref[rows, :],
                     cat(pkt_ref, rows, kt_ref, rows), cat(pvt_ref, rows, vt_ref, rows), rows)

    kinds = [
        _TileKind(1, 0, band(w1, d1) & has_prev, b1_first),
        _TileKind(nb1 - 1, 0, band(w1, d1), b1_later),
        _TileKind(d2, 1, band(w2, d2) & has_prev, b2_first),
        _TileKind(d2 * (nb2 - 1), 1, band(w2, d2), b2_later),
        _TileKind(d3, 2, band(w3, d3) & has_prev, b3_all),
    ]
    scale = HEAD_DIM ** -0.5

    for kind in kinds:
        def scores(idxs, kind=kind):
            for t in [kind.tile(i) for i in idxs]:
                s = lax.dot_general(t.q().astype(BF16), t.k().astype(BF16),
                                    (((1,), (1,)), ((), ())), preferred_element_type=F32)
                s_ref[t.sp_rows, :] = jnp.where(kind.mask, s * scale, MASK_VALUE)
        _for_groups(kind.n, max_group, scores)

    for kind in kinds:
        def softmax(idxs, kind=kind):
            for t in [kind.tile(i) for i in idxs]:
                s = s_ref[t.sp_rows, :]
                m = jnp.max(s, axis=-1, keepdims=True)
                p_ref[t.sp_rows, :] = jnp.exp(s - m).astype(p_ref.dtype)
                maxs[kind.branch][t.dst_rows, :] = jnp.broadcast_to(m, (BLOCK, HEAD_DIM))
        _for_groups(kind.n, max_group, softmax)

    ones = jnp.ones((2 * BLOCK, HEAD_DIM), BF16)
    for kind in kinds:
        def values(idxs, kind=kind):
            for t in [kind.tile(i) for i in idxs]:
                v1 = jnp.concatenate([t.v().astype(BF16), ones], axis=1)
                o = jnp.dot(p_ref[t.sp_rows, :], v1, preferred_element_type=F32)
                nums[kind.branch][t.dst_rows, :] = o[:, :HEAD_DIM]
                dens[kind.branch][t.dst_rows, :] = o[:, HEAD_DIM:]
        _for_groups(kind.n, max_group, values)

    def merge(it, carry):
        for r2 in range(d2):
            res = pl.ds(pl.multiple_of(r2 * res_rows + it * merge_chunk, merge_chunk), merge_chunk)
            nat = pl.ds(r2 + it * merge_chunk * d2, merge_chunk, stride=d2)
            rows = (nat, res, res)
            ms = [maxs[b][rows[b], :] for b in range(3)]
            top = jnp.maximum(jnp.maximum(ms[0], ms[1]), ms[2])
            ws = [jnp.exp(m - top) for m in ms]
            numer = ws[0] * nums[0][rows[0], :]
            denom = ws[0] * dens[0][rows[0], :]
            for b in (1, 2):
                numer = numer + ws[b] * nums[b][rows[b], :]
                denom = denom + ws[b] * dens[b][rows[b], :]
            out_ref[nat, :] = numer * (1.0 / denom)
        return carry

    lax.fori_loop(0, res_rows // merge_chunk, merge, 0)

    pkt_ref[...] = kt_ref[...]
    pvt_ref[...] = vt_ref[...]
    pk_tail_ref[...] = kc_ref[pl.ds(rows_total - BLOCK, BLOCK), :]
    pv_tail_ref[...] = vc_ref[pl.ds(rows_total - BLOCK, BLOCK), :]


def _attention(qkv):
    s_len = qkv.shape[0]
    assert s_len % ATTN_ROWS == 0 and len(DILATED_BRANCHES) == 3
    n_tiles = sum(ATTN_ROWS // BLOCK for _ in DILATED_BRANCHES)
    blk = (ATTN_ROWS, HEAD_DIM)

    def spec(part):
        return pl.BlockSpec(blk, lambda h, m: (m, part * N_HEADS + h))

    return pl.pallas_call(
        functools.partial(_attn_kernel, branches=DILATED_BRANCHES, max_group=16, merge_chunk=32),
        grid=(N_HEADS, s_len // ATTN_ROWS),
        in_specs=[spec(0), spec(1), spec(2)],
        out_specs=pl.BlockSpec(blk, lambda h, m: (m, h)),
        out_shape=jax.ShapeDtypeStruct((s_len, ATTN_WIDTH), F32),
        scratch_shapes=([pltpu.VMEM(blk, F32) for _ in range(5)]
                        + [pltpu.VMEM((BLOCK, HEAD_DIM), F32) for _ in range(2)]
                        + [pltpu.VMEM((n_tiles * BLOCK, 2 * BLOCK), F32),
                           pltpu.VMEM((n_tiles * BLOCK, 2 * BLOCK), BF16)]
                        + [pltpu.VMEM(blk, F32) for _ in range(9)]),
        compiler_params=_params(("arbitrary", "arbitrary"), 48),
        name="attention",
    )(qkv, qkv, qkv)


def _glu_conv_kernel(a_ref, rs_ref, wa_ref, wg_ref, cw_ref, cb_ref, out_ref,
                     halo_ref, shift_ref, *hbufs, row_chunk):
    i, j = pl.program_id(0), pl.program_id(1)
    tm, tn = out_ref.shape

    def halo_cols(n):
        return pl.ds(pl.multiple_of(j * tn + n * LANES, LANES), LANES)

    @pl.when(i == 0)
    def _():
        for hbuf in hbufs:
            hbuf[0:CONV_HALO, :] = jnp.zeros((CONV_HALO, LANES), F32)

    @pl.when(i > 0)
    def _():
        for n, hbuf in enumerate(hbufs):
            hbuf[0:CONV_HALO, :] = halo_ref[:, halo_cols(n)]

    x, row_scale = a_ref[...], rs_ref[...]
    first = CONV_HALO - (CONV_WIDTH - 1)
    ext = shift_ref.shape[2]
    for c0 in range(0, tn, MXU_COLS):
        chunk = slice(c0, c0 + MXU_COLS)
        pa = jnp.dot(x, wa_ref[:, chunk], preferred_element_type=F32)
        pg = jnp.dot(x, wg_ref[:, chunk], preferred_element_type=F32)
        for t, cols in enumerate(_lane_tiles(MXU_COLS)):
            lanes = slice(c0 + cols.start, c0 + cols.stop)
            hbuf = hbufs[lanes.start // LANES]
            hbuf[CONV_HALO:, :] = ((pa[:, cols] * row_scale)
                                   * jax.nn.sigmoid(pg[:, cols] * row_scale))
            for s in range(1, SUBLANES):
                shift_ref[t, s - 1, :, :] = hbuf[pl.ds(s, ext), :]
            w = cw_ref[:, lanes]
            b = cb_ref[:, lanes]
            for base in range(0, tm, row_chunk):
                acc = None
                for tap in range(CONV_WIDTH):
                    off = first + tap
                    s, aligned = off % SUBLANES, off - off % SUBLANES
                    if s == 0:
                        rows = hbuf[pl.ds(base + aligned, row_chunk), :]
                    else:
                        rows = shift_ref[t, s - 1, pl.ds(base + aligned, row_chunk), :]
                    term = w[tap:tap + 1, :] * rows
                    acc = term if acc is None else acc + term
                out_ref[pl.ds(base, row_chunk), lanes] = acc + b
    for n, hbuf in enumerate(hbufs):
        halo_ref[:, halo_cols(n)] = hbuf[tm:tm + CONV_HALO, :]


def _glu_conv(a, rs, w, off_a, off_g, conv_w, conv_b, layer, tm=512, tn=1024):
    m, k = a.shape
    c = CONV_CHANNELS
    oa, og = off_a // tn, off_g // tn
    return pl.pallas_call(
        functools.partial(_glu_conv_kernel, row_chunk=64),
        grid=(m // tm, c // tn),
        in_specs=[pl.BlockSpec((tm, k), lambda i, j: (i, 0)),
                  pl.BlockSpec((tm, LANES), lambda i, j: (i, 0)),
                  pl.BlockSpec((k, tn), lambda i, j: (0, oa + j)),
                  pl.BlockSpec((k, tn), lambda i, j: (0, og + j)),
                  pl.BlockSpec((None, CONV_WIDTH, tn), lambda i, j: (layer, 0, j)),
                  pl.BlockSpec((None, 1, tn), lambda i, j: (layer, 0, j))],
        out_specs=pl.BlockSpec((tm, tn), lambda i, j: (i, j)),
        out_shape=jax.ShapeDtypeStruct((m, c), F32),
        scratch_shapes=([pltpu.VMEM((CONV_HALO, c), F32),
                         pltpu.VMEM((MXU_COLS // LANES, SUBLANES - 1, tm + CONV_HALO - SUBLANES,
                                     LANES), F32)]
                        + [pltpu.VMEM((tm + CONV_HALO, LANES), F32) for _ in range(tn // LANES)]),
        compiler_params=_params(("arbitrary", "arbitrary"), 56),
        name="glu_conv",
    )(a, rs, w, w, conv_w, conv_b)


def _conv_norm_kernel(x_ref, lng_ref, lnb_ref, g_ref, out_ref, *, ln_chunk):
    def norm_rows(rc, carry):
        rows = pl.ds(pl.multiple_of(rc * ln_chunk, ln_chunk), ln_chunk)
        x = x_ref[rows, :]
        mu = jnp.mean(x, axis=-1, keepdims=True)
        xc = x - mu
        var = jnp.mean(xc * xc, axis=-1, keepdims=True)
        y = xc * lax.rsqrt(var + LN_EPS) * lng_ref[...] + lnb_ref[...]
        y = y * jax.nn.sigmoid(y)
        ms = jnp.mean(y * y, axis=-1, keepdims=True)
        out_ref[rows, :] = (y * lax.rsqrt(ms + RMS_EPS) * g_ref[...]).astype(out_ref.dtype)
        return carry

    lax.fori_loop(0, x_ref.shape[0] // ln_chunk, norm_rows, 0, unroll=4)


def _conv_norm(x, ln_g, ln_b, out_g, layer, ts=256):
    s_len, c = x.shape
    vec_spec = pl.BlockSpec((None, 1, c), lambda i: (layer, 0, 0))
    return pl.pallas_call(
        functools.partial(_conv_norm_kernel, ln_chunk=16),
        grid=(s_len // ts,),
        in_specs=[pl.BlockSpec((ts, c), lambda i: (i, 0)), vec_spec, vec_spec, vec_spec],
        out_specs=pl.BlockSpec((ts, c), lambda i: (i, 0)),
        out_shape=jax.ShapeDtypeStruct((s_len, c), BF16),
        compiler_params=_params(("parallel",), 32),
        name="conv_norm",
    )(x, ln_g, ln_b, out_g)


def _rotary_tables(seq):
    pos = jnp.arange(seq, dtype=F32)
    inv_freq = ROPE_THETA ** (-(jnp.arange(0, ROT_DIM, 2, dtype=F32) / ROT_DIM))
    ang = pos[:, None] * inv_freq[None, :]
    cos, sin = jnp.cos(ang), jnp.sin(ang)
    half = ROT_DIM // 2
    rest = HEAD_DIM - ROT_DIM
    ones = jnp.ones((seq, rest), F32)
    zeros_h = jnp.zeros((seq, half), F32)
    zeros_r = jnp.zeros((seq, rest), F32)
    cos_t = jnp.concatenate([cos, cos, ones], axis=-1)
    sin_lo = jnp.concatenate([-sin, zeros_h, zeros_r], axis=-1)
    sin_hi = jnp.concatenate([zeros_h, sin, zeros_r], axis=-1)
    return cos_t, sin_lo, sin_hi


def _cast_block(w, n_steps):
    _, rows, cols = w.shape
    for br, bc in ((512, 512), (512, 1024), (1024, 1024)):
        if rows % br == 0 and cols % bc == 0 and (rows // br) * (cols // bc) <= n_steps:
            return br, bc
    raise ValueError("no cast block for %s in %d steps" % (w.shape, n_steps))


def kernel(x, ffn1_norm, ffn1_w_gate, ffn1_w_up, ffn1_w_down, mix_norm, w_in, conv_w, conv_b, conv_ln_g, conv_ln_b, attn_out_norm, conv_out_norm, w_out, ffn2_norm, ffn2_w_gate, ffn2_w_up, ffn2_w_down, final_norm):
    b, s_len, d = x.shape
    assert b == 1 and d == D_MODEL
    x = x.reshape(s_len, d)
    rot = _rotary_tables(s_len)
    a, c = ATTN_WIDTH, CONV_CHANNELS

    def vec(p):
        return p.reshape(p.shape[0], 1, p.shape[1])

    ffn1_norm, mix_norm, ffn2_norm = vec(ffn1_norm), vec(mix_norm), vec(ffn2_norm)
    conv_b, conv_ln_g, conv_ln_b = vec(conv_b), vec(conv_ln_g), vec(conv_ln_b)
    attn_out_norm, conv_out_norm = vec(attn_out_norm), vec(conv_out_norm)
    host_steps = (s_len // 1024) * (D_FF // 512)

    def cast(w, layer):
        return w, layer, _cast_block(w, host_steps)

    gate_b, up_b = ffn1_w_gate[0].astype(BF16), ffn1_w_up[0].astype(BF16)
    xg, rs = _prenorm(x, ffn1_norm, 0)
    for l in range(DEPTH):
        last = l + 1 == DEPTH
        h, (down_b, w_in_b, w_out_b) = _swiglu_mm(
            xg, rs, gate_b, up_b,
            casts=[cast(ffn1_w_down, l), cast(w_in, l), cast(w_out, l)])
        x, xg, rs = _res_mm([h], down_b, x, FFN_RESIDUAL_SCALE, next_norm=(mix_norm, l))
        qkv, (gate_b, up_b) = _qkv_mm(xg, rs, w_in_b, rot,
                                      casts=[cast(ffn2_w_gate, l), cast(ffn2_w_up, l)])
        conv = _glu_conv(xg, rs, w_in_b, 3 * a, 3 * a + c, conv_w, conv_b, l)
        attn_n = _rmsnorm(_attention(qkv), attn_out_norm, l, BF16)
        conv_n = _conv_norm(conv, conv_ln_g, conv_ln_b, conv_out_norm, l)
        x, xg, rs = _res_mm([attn_n, conv_n], w_out_b, x, 1.0, next_norm=(ffn2_norm, l))
        next_casts = [] if last else [cast(ffn1_w_gate, l + 1), cast(ffn1_w_up, l + 1)]
        h, cast_outs = _swiglu_mm(xg, rs, gate_b, up_b,
                                  casts=[cast(ffn2_w_down, l)] + next_casts)
        if last:
            x, = _res_mm([h], cast_outs[0], x, FFN_RESIDUAL_SCALE)
        else:
            gate_b, up_b = cast_outs[1:]
            x, xg, rs = _res_mm([h], cast_outs[0], x, FFN_RESIDUAL_SCALE,
                                next_norm=(ffn1_norm, l + 1))
    out = _rmsnorm(x, final_norm.reshape(1, 1, d), 0, F32)
    return out.reshape(b, s_len, d)
```

```python
import functools
from typing import Any, Callable, NamedTuple

import jax
import jax.numpy as jnp
from jax import lax
from jax.experimental import pallas as pl
from jax.experimental.pallas import tpu as pltpu

D_MODEL = 4096
DEPTH = 4
N_HEADS = 16
HEAD_DIM = 128
ATTN_WIDTH = N_HEADS * HEAD_DIM
CONV_CHANNELS = D_MODEL - ATTN_WIDTH
IN_WIDTH = 3 * ATTN_WIDTH + 2 * CONV_CHANNELS
CONV_WIDTH = 31
D_FF = (3 * D_MODEL) // 2
ROPE_THETA = 500000.0
ROT_DIM = HEAD_DIM // 4
DILATED_BRANCHES = ((128, 1), (512, 4), (2048, 16))
BLOCK = 128
RMS_EPS = 1e-5
LN_EPS = 1e-5
FFN_RESIDUAL_SCALE = 0.5
MASK_VALUE = -1e30

LANES = 128
SUBLANES = 8
MXU_COLS = 256
CONV_HALO = 32
ATTN_ROWS = BLOCK * max(d for _, d in DILATED_BRANCHES)
MIB = 1024 * 1024
NORM_BLOCK_BYTES = 4 * MIB

F32 = jnp.float32
BF16 = jnp.bfloat16


def _params(semantics, vmem_mib):
    return pltpu.CompilerParams(dimension_semantics=semantics,
                                vmem_limit_bytes=vmem_mib * MIB)


def _rmsnorm_kernel(x_ref, g_ref, o_ref):
    x = x_ref[...]
    ms = jnp.mean(x * x, axis=-1, keepdims=True)
    o_ref[...] = (x * lax.rsqrt(ms + RMS_EPS) * g_ref[...]).astype(o_ref.dtype)


def _rmsnorm(x, g, layer, out_dtype):
    m, d = x.shape
    tr = NORM_BLOCK_BYTES // (4 * d)
    return pl.pallas_call(
        _rmsnorm_kernel,
        grid=(m // tr,),
        in_specs=[pl.BlockSpec((tr, d), lambda i: (i, 0)),
                  pl.BlockSpec((None, 1, d), lambda i: (layer, 0, 0))],
        out_specs=pl.BlockSpec((tr, d), lambda i: (i, 0)),
        out_shape=jax.ShapeDtypeStruct((m, d), out_dtype),
        compiler_params=_params(("parallel",), 32),
        name="rmsnorm",
    )(x, g)


def _prenorm_kernel(x_ref, g_ref, xg_ref, rs_ref):
    x = x_ref[...]
    ms = jnp.mean(x * x, axis=-1, keepdims=True)
    xg_ref[...] = (x * g_ref[...]).astype(xg_ref.dtype)
    rs_ref[...] = jnp.broadcast_to(lax.rsqrt(ms + RMS_EPS), rs_ref.shape)


def _prenorm(x, g, layer, tr=256):
    m, d = x.shape
    return pl.pallas_call(
        _prenorm_kernel,
        grid=(m // tr,),
        in_specs=[pl.BlockSpec((tr, d), lambda i: (i, 0)),
                  pl.BlockSpec((None, 1, d), lambda i: (layer, 0, 0))],
        out_specs=[pl.BlockSpec((tr, d), lambda i: (i, 0)),
                   pl.BlockSpec((tr, LANES), lambda i: (i, 0))],
        out_shape=[jax.ShapeDtypeStruct((m, d), BF16), jax.ShapeDtypeStruct((m, LANES), F32)],
        compiler_params=_params(("parallel",), 32),
        name="prenorm",
    )(x, g)


def _cast_plan(w, layer, block, n_steps_j):
    _, rows, cols = w.shape
    br, bc = block
    nbc = cols // bc
    n_blocks = (rows // br) * nbc

    def block_index(i, j):
        s = jnp.minimum(i * n_steps_j + j, n_blocks - 1)
        return s // nbc, s % nbc

    in_spec = pl.BlockSpec((None, br, bc), lambda i, j: (layer,) + block_index(i, j))
    out_spec = pl.BlockSpec((br, bc), block_index)
    return in_spec, out_spec, jax.ShapeDtypeStruct((rows, cols), BF16), n_blocks


def _run_casts(src_refs, dst_refs, n_blocks_list, n_steps):
    step = pl.program_id(0) * pl.num_programs(1) + pl.program_id(1)
    for src, dst, n_blocks in zip(src_refs, dst_refs, n_blocks_list):
        if n_blocks == n_steps:
            dst[...] = src[...].astype(dst.dtype)
        else:
            @pl.when(step < n_blocks)
            def _(src=src, dst=dst):
                dst[...] = src[...].astype(dst.dtype)


def _host_call(body, grid, in_specs, out_specs, out_shapes, operands, casts, vmem_mib, name,
               scratch_shapes=()):
    n_in, n_out, n_cast = len(in_specs), len(out_specs), len(casts)
    plans = [_cast_plan(w, layer, block, grid[1]) for w, layer, block in casts]
    n_steps = grid[0] * grid[1]
    assert all(p[3] <= n_steps for p in plans)
    n_blocks_list = [p[3] for p in plans]

    def kernel_fn(*refs):
        ins, cast_ins = refs[:n_in], refs[n_in:n_in + n_cast]
        outs = refs[n_in + n_cast:n_in + n_cast + n_out]
        cast_outs = refs[n_in + n_cast + n_out:n_in + 2 * n_cast + n_out]
        scratch = refs[n_in + 2 * n_cast + n_out:]
        _run_casts(cast_ins, cast_outs, n_blocks_list, n_steps)
        body(ins, outs, scratch)

    results = pl.pallas_call(
        kernel_fn,
        grid=grid,
        in_specs=list(in_specs) + [p[0] for p in plans],
        out_specs=list(out_specs) + [p[1] for p in plans],
        out_shape=list(out_shapes) + [p[2] for p in plans],
        scratch_shapes=list(scratch_shapes),
        compiler_params=_params(("arbitrary", "arbitrary"), vmem_mib),
        name=name,
    )(*operands, *[w for w, _, _ in casts])
    return results[:n_out], results[n_out:]


def _lane_tiles(width):
    return [slice(c * LANES, (c + 1) * LANES) for c in range(width // LANES)]


def _swiglu_mm(a, rs, w1, w2, casts=(), tm=1024, tn=512):
    m, k = a.shape
    n_out = w1.shape[1]

    def body(ins, outs, scratch):
        a_ref, rs_ref, w1_ref, w2_ref = ins
        o_ref, = outs
        x = a_ref[...]
        row_scale = rs_ref[...]
        for c0 in range(0, tn, MXU_COLS):
            chunk = slice(c0, c0 + MXU_COLS)
            p1 = jnp.dot(x, w1_ref[:, chunk], preferred_element_type=F32)
            p2 = jnp.dot(x, w2_ref[:, chunk], preferred_element_type=F32)
            for cols in _lane_tiles(MXU_COLS):
                g1, g2 = p1[:, cols] * row_scale, p2[:, cols] * row_scale
                o = (g1 * jax.nn.sigmoid(g1)) * g2
                o_ref[:, c0 + cols.start:c0 + cols.stop] = o.astype(o_ref.dtype)

    (out,), cast_outs = _host_call(
        body, (m // tm, n_out // tn),
        [pl.BlockSpec((tm, k), lambda i, j: (i, 0)),
         pl.BlockSpec((tm, LANES), lambda i, j: (i, 0)),
         pl.BlockSpec((k, tn), lambda i, j: (0, j)),
         pl.BlockSpec((k, tn), lambda i, j: (0, j))],
        [pl.BlockSpec((tm, tn), lambda i, j: (i, j))],
        [jax.ShapeDtypeStruct((m, n_out), BF16)],
        (a, rs, w1, w2), list(casts), 60, "swiglu_mm")
    return out, cast_outs


def _res_mm(a_list, w, res, scale, next_norm=None, tm=1024, tn=512):
    n_a = len(a_list)
    m, k = a_list[0].shape
    n = w.shape[-1]
    n_j = n // tn

    def body(ins, outs, scratch):
        a_refs, w_refs, res_ref = ins[:n_a], ins[n_a:2 * n_a], ins[2 * n_a]
        operands = [a_ref[...] for a_ref in a_refs]
        part = None
        for c0 in range(0, tn, MXU_COLS):
            chunk = slice(c0, c0 + MXU_COLS)
            acc = None
            for a, w_ref in zip(operands, w_refs):
                d = jnp.dot(a, w_ref[:, chunk], preferred_element_type=F32)
                acc = d if acc is None else acc + d
            if scale != 1.0:
                acc = scale * acc
            x = res_ref[:, chunk] + acc
            outs[0][:, chunk] = x
            if next_norm is None:
                continue
            outs[1][:, chunk] = (x * ins[2 * n_a + 1][:, chunk]).astype(outs[1].dtype)
            for cols in _lane_tiles(MXU_COLS):
                sq = x[:, cols] * x[:, cols]
                part = sq if part is None else part + sq
        if next_norm is None:
            return
        rs_ref = outs[2]
        ssq_ref, = scratch
        j = pl.program_id(1)

        @pl.when(j == 0)
        def _():
            ssq_ref[...] = part

        @pl.when(j > 0)
        def _():
            ssq_ref[...] += part

        @pl.when(j == n_j - 1)
        def _():
            ms = jnp.sum(ssq_ref[...], axis=-1, keepdims=True) * (1.0 / n)
            rs_ref[...] = jnp.broadcast_to(lax.rsqrt(ms + RMS_EPS), rs_ref.shape)

    in_specs = ([pl.BlockSpec((tm, k), lambda i, j: (i, 0)) for _ in a_list]
                + [pl.BlockSpec((k, tn), functools.partial(lambda i, j, s: (s, j), s=s))
                   for s in range(n_a)]
                + [pl.BlockSpec((tm, tn), lambda i, j: (i, j))])
    out_specs = [pl.BlockSpec((tm, tn), lambda i, j: (i, j))]
    out_shapes = [jax.ShapeDtypeStruct((m, n), F32)]
    operands = list(a_list) + [w] * n_a + [res]
    scratch_shapes = []
    if next_norm is not None:
        g, layer = next_norm
        in_specs.append(pl.BlockSpec((None, 1, tn), lambda i, j: (layer, 0, j)))
        operands.append(g)
        out_specs += [pl.BlockSpec((tm, tn), lambda i, j: (i, j)),
                      pl.BlockSpec((tm, LANES), lambda i, j: (i, 0))]
        out_shapes += [jax.ShapeDtypeStruct((m, n), BF16), jax.ShapeDtypeStruct((m, LANES), F32)]
        scratch_shapes.append(pltpu.VMEM((tm, LANES), F32))
    outs, _ = _host_call(body, (m // tm, n_j), in_specs, out_specs, out_shapes, operands, [],
                         60, "res_mm", scratch_shapes)
    return outs


def _qkv_mm(a, rs, w, rot, casts=(), tm=1024, tn=512):
    m, k = a.shape
    n_out = 3 * ATTN_WIDTH
    n_rot_tiles = 2 * ATTN_WIDTH // tn
    tab_spec = pl.BlockSpec((tm, HEAD_DIM), lambda i, j: (i, 0))

    def body(ins, outs, scratch):
        a_ref, rs_ref, w_ref, cos_ref, sin_lo_ref, sin_hi_ref = ins
        o_ref, = outs
        j = pl.program_id(1)

        def project(rotate):
            a, row_scale = a_ref[...], rs_ref[...]
            for c0 in range(0, tn, MXU_COLS):
                p = jnp.dot(a, w_ref[:, c0:c0 + MXU_COLS], preferred_element_type=F32)
                for cols in _lane_tiles(MXU_COLS):
                    x = p[:, cols] * row_scale
                    if rotate:
                        x = (x * cos_ref[...]
                             + pltpu.roll(x, HEAD_DIM - ROT_DIM // 2, 1) * sin_lo_ref[...]
                             + pltpu.roll(x, ROT_DIM // 2, 1) * sin_hi_ref[...])
                    o_ref[:, c0 + cols.start:c0 + cols.stop] = x

        pl.when(j < n_rot_tiles)(lambda: project(True))
        pl.when(j >= n_rot_tiles)(lambda: project(False))

    (out,), cast_outs = _host_call(
        body, (m // tm, n_out // tn),
        [pl.BlockSpec((tm, k), lambda i, j: (i, 0)),
         pl.BlockSpec((tm, LANES), lambda i, j: (i, 0)),
         pl.BlockSpec((k, tn), lambda i, j: (0, j)),
         tab_spec, tab_spec, tab_spec],
        [pl.BlockSpec((tm, tn), lambda i, j: (i, j))],
        [jax.ShapeDtypeStruct((m, n_out), F32)],
        (a, rs, w) + tuple(rot), list(casts), 60, "qkv_mm")
    return out, cast_outs


class _Tile(NamedTuple):
    sp_rows: Any
    q: Callable
    k: Callable
    v: Callable
    dst_rows: Any


class _TileKind(NamedTuple):
    n: int
    branch: int
    mask: Any
    tile: Callable


def _aligned(x, multiple):
    return x if isinstance(x, int) else pl.multiple_of(x, multiple)


def _for_groups(n, max_group, fn):
    group = max(g for g in range(1, max_group + 1) if n % g == 0)
    if n == group:
        fn(list(range(group)))
        return

    def body(it, carry):
        fn([it * group + g for g in range(group)])
        return carry

    lax.fori_loop(0, n // group, body, 0)


def _attn_kernel(q_ref, kc_ref, vc_ref, out_ref, qt_ref, kt_ref, vt_ref, pkt_ref, pvt_ref,
                 pk_tail_ref, pv_tail_ref, s_ref, p_ref, *merge_bufs, branches, max_group,
                 merge_chunk):
    (w1, d1), (w2, d2), (w3, d3) = branches
    rows_total, width = q_ref.shape
    assert width == HEAD_DIM and d1 == 1 and d3 % d2 == 0 and rows_total == BLOCK * d3
    e = d3 // d2
    res_rows = rows_total // d2
    nb1 = rows_total // BLOCK
    nb2 = res_rows // BLOCK
    nums, maxs, dens = merge_bufs[0::3], merge_bufs[1::3], merge_bufs[2::3]
    step = pl.program_id(1)

    @pl.when(step == 0)
    def _():
        for ref in (pkt_ref, pvt_ref, pk_tail_ref, pv_tail_ref):
            ref[...] = jnp.zeros(ref.shape, ref.dtype)

    for r2 in range(d2):
        src = pl.ds(r2, res_rows, stride=d2)
        dst = pl.ds(r2 * res_rows, res_rows)
        qt_ref[dst, :] = q_ref[src, :]
        kt_ref[dst, :] = kc_ref[src, :]
        vt_ref[dst, :] = vc_ref[src, :]

    qi = lax.broadcasted_iota(jnp.int32, (BLOCK, 2 * BLOCK), 0) + BLOCK
    kj = lax.broadcasted_iota(jnp.int32, (BLOCK, 2 * BLOCK), 1)
    delta = qi - kj
    has_prev = (kj >= BLOCK) | (step > 0)

    def band(window, d):
        return (delta >= 0) & (delta <= window // d)

    def sp_rows(t):
        return pl.ds(_aligned(t * BLOCK, BLOCK), BLOCK)

    def cat(prev_ref, prev_rows, cur_ref, cur_rows):
        return lambda: jnp.concatenate([prev_ref[prev_rows, :], cur_ref[cur_rows, :]], axis=0)

    def b1_first(idx):
        rows = pl.ds(0, BLOCK)
        return _Tile(sp_rows(0), lambda: q_ref[rows, :],
                     cat(pk_tail_ref, slice(None), kc_ref, rows),
                     cat(pv_tail_ref, slice(None), vc_ref, rows), rows)

    def b1_later(idx):
        base = _aligned((idx + 1) * BLOCK, BLOCK)
        rows, kv_rows = pl.ds(base, BLOCK), pl.ds(base - BLOCK, 2 * BLOCK)
        return _Tile(sp_rows(idx + 1), lambda: q_ref[rows, :],
                     lambda: kc_ref[kv_rows, :], lambda: vc_ref[kv_rows, :], rows)

    def b2_first(r2):
        base = _aligned(r2 * res_rows, BLOCK)
        rows, prev_rows = pl.ds(base, BLOCK), pl.ds(base + res_rows - BLOCK, BLOCK)
        return _Tile(sp_rows(nb1 + r2 * nb2), lambda: qt_ref[rows, :],
                     cat(pkt_ref, prev_rows, kt_ref, rows),
                     cat(pvt_ref, prev_rows, vt_ref, rows), rows)

    def b2_later(idx):
        r2, b = idx % d2, idx // d2 + 1
        base = _aligned(r2 * res_rows + b * BLOCK, BLOCK)
        rows, kv_rows = pl.ds(base, BLOCK), pl.ds(base - BLOCK, 2 * BLOCK)
        return _Tile(sp_rows(nb1 + r2 * nb2 + b), lambda: qt_ref[rows, :],
                     lambda: kt_ref[kv_rows, :], lambda: vt_ref[kv_rows, :], rows)

    def b3_all(idx):
        r2, off = idx % d2, idx // d2
        rows = pl.ds(r2 * res_rows + off, BLOCK, stride=e)
        return _Tile(sp_rows(nb1 + d2 * nb2 + idx), lambda: qt_ref[rows, :],
                     cat(pkt_ref, rows, kt_ref, rows), cat(pvt_ref, rows, vt_ref, rows), rows)

    kinds = [
        _TileKind(1, 0, band(w1, d1) & has_prev, b1_first),
        _TileKind(nb1 - 1, 0, band(w1, d1), b1_later),
        _TileKind(d2, 1, band(w2, d2) & has_prev, b2_first),
        _TileKind(d2 * (nb2 - 1), 1, band(w2, d2), b2_later),
        _TileKind(d3, 2, band(w3, d3) & has_prev, b3_all),
    ]
    scale = HEAD_DIM ** -0.5

    for kind in kinds:
        def scores(idxs, kind=kind):
            for t in [kind.tile(i) for i in idxs]:
                s = lax.dot_general(t.q().astype(BF16), t.k().astype(BF16),
                                    (((1,), (1,)), ((), ())), preferred_element_type=F32)
                s_ref[t.sp_rows, :] = jnp.where(kind.mask, s * scale, MASK_VALUE)
        _for_groups(kind.n, max_group, scores)

    for kind in kinds:
        def softmax(idxs, kind=kind):
            for t in [kind.tile(i) for i in idxs]:
                s = s_ref[t.sp_rows, :]
                m = jnp.max(s, axis=-1, keepdims=True)
                p_ref[t.sp_rows, :] = jnp.exp(s - m).astype(p_ref.dtype)
                maxs[kind.branch][t.dst_rows, :] = jnp.broadcast_to(m, (BLOCK, HEAD_DIM))
        _for_groups(kind.n, max_group, softmax)

    ones = jnp.ones((2 * BLOCK, HEAD_DIM), BF16)
    for kind in kinds:
        def values(idxs, kind=kind):
            for t in [kind.tile(i) for i in idxs]:
                v1 = jnp.concatenate([t.v().astype(BF16), ones], axis=1)
                o = jnp.dot(p_ref[t.sp_rows, :], v1, preferred_element_type=F32)
                nums[kind.branch][t.dst_rows, :] = o[:, :HEAD_DIM]
                dens[kind.branch][t.dst_rows, :] = o[:, HEAD_DIM:]
        _for_groups(kind.n, max_group, values)

    def merge(it, carry):
        for r2 in range(d2):
            res = pl.ds(pl.multiple_of(r2 * res_rows + it * merge_chunk, merge_chunk), merge_chunk)
            nat = pl.ds(r2 + it * merge_chunk * d2, merge_chunk, stride=d2)
            rows = (nat, res, res)
            ms = [maxs[b][rows[b], :] for b in range(3)]
            top = jnp.maximum(jnp.maximum(ms[0], ms[1]), ms[2])
            ws = [jnp.exp(m - top) for m in ms]
            numer = ws[0] * nums[0][rows[0], :]
            denom = ws[0] * dens[0][rows[0], :]
            for b in (1, 2):
                numer = numer + ws[b] * nums[b][rows[b], :]
                denom = denom + ws[b] * dens[b][rows[b], :]
            out_ref[nat, :] = numer * (1.0 / denom)
        return carry

    lax.fori_loop(0, res_rows // merge_chunk, merge, 0)

    pkt_ref[...] = kt_ref[...]
    pvt_ref[...] = vt_ref[...]
    pk_tail_ref[...] = kc_ref[pl.ds(rows_total - BLOCK, BLOCK), :]
    pv_tail_ref[...] = vc_ref[pl.ds(rows_total - BLOCK, BLOCK), :]


def _attention(qkv):
    s_len = qkv.shape[0]
    assert s_len % ATTN_ROWS == 0 and len(DILATED_BRANCHES) == 3
    n_tiles = sum(ATTN_ROWS // BLOCK for _ in DILATED_BRANCHES)
    blk = (ATTN_ROWS, HEAD_DIM)

    def spec(part):
        return pl.BlockSpec(blk, lambda h, m: (m, part * N_HEADS + h))

    return pl.pallas_call(
        functools.partial(_attn_kernel, branches=DILATED_BRANCHES, max_group=16, merge_chunk=32),
        grid=(N_HEADS, s_len // ATTN_ROWS),
        in_specs=[spec(0), spec(1), spec(2)],
        out_specs=pl.BlockSpec(blk, lambda h, m: (m, h)),
        out_shape=jax.ShapeDtypeStruct((s_len, ATTN_WIDTH), F32),
        scratch_shapes=([pltpu.VMEM(blk, F32) for _ in range(5)]
                        + [pltpu.VMEM((BLOCK, HEAD_DIM), F32) for _ in range(2)]
                        + [pltpu.VMEM((n_tiles * BLOCK, 2 * BLOCK), F32),
                           pltpu.VMEM((n_tiles * BLOCK, 2 * BLOCK), BF16)]
                        + [pltpu.VMEM(blk, F32) for _ in range(9)]),
        compiler_params=_params(("arbitrary", "arbitrary"), 48),
        name="attention",
    )(qkv, qkv, qkv)


def _glu_conv_kernel(a_ref, rs_ref, wa_ref, wg_ref, cw_ref, cb_ref, out_ref,
                     halo_ref, shift_ref, *hbufs, row_chunk):
    i = pl.program_id(1)
    tm, tn = out_ref.shape

    def halo_cols(n):
        return slice(n * LANES, (n + 1) * LANES)

    @pl.when(i == 0)
    def _():
        for hbuf in hbufs:
            hbuf[0:CONV_HALO, :] = jnp.zeros((CONV_HALO, LANES), F32)

    @pl.when(i > 0)
    def _():
        for n, hbuf in enumerate(hbufs):
            hbuf[0:CONV_HALO, :] = halo_ref[:, halo_cols(n)]

    x, row_scale = a_ref[...], rs_ref[...]
    first = CONV_HALO - (CONV_WIDTH - 1)
    ext = shift_ref.shape[2]
    for c0 in range(0, tn, MXU_COLS):
        chunk = slice(c0, c0 + MXU_COLS)
        pa = jnp.dot(x, wa_ref[:, chunk], preferred_element_type=F32)
        pg = jnp.dot(x, wg_ref[:, chunk], preferred_element_type=F32)
        for t, cols in enumerate(_lane_tiles(MXU_COLS)):
            lanes = slice(c0 + cols.start, c0 + cols.stop)
            hbuf = hbufs[lanes.start // LANES]
            hbuf[CONV_HALO:, :] = ((pa[:, cols] * row_scale)
                                   * jax.nn.sigmoid(pg[:, cols] * row_scale))
            for s in range(1, SUBLANES):
                shift_ref[t, s - 1, :, :] = hbuf[pl.ds(s, ext), :]
            w = cw_ref[:, lanes]
            b = cb_ref[:, lanes]
            for base in range(0, tm, row_chunk):
                acc = None
                for tap in range(CONV_WIDTH):
                    off = first + tap
                    s, aligned = off % SUBLANES, off - off % SUBLANES
                    if s == 0:
                        rows = hbuf[pl.ds(base + aligned, row_chunk), :]
                    else:
                        rows = shift_ref[t, s - 1, pl.ds(base + aligned, row_chunk), :]
                    term = w[tap:tap + 1, :] * rows
                    acc = term if acc is None else acc + term
                out_ref[pl.ds(base, row_chunk), lanes] = acc + b
    for n, hbuf in enumerate(hbufs):
        halo_ref[:, halo_cols(n)] = hbuf[tm:tm + CONV_HALO, :]


def _glu_conv(a, rs, w, off_a, off_g, conv_w, conv_b, layer, tm=512, tn=1024):
    m, k = a.shape
    c = CONV_CHANNELS
    oa, og = off_a // tn, off_g // tn
    return pl.pallas_call(
        functools.partial(_glu_conv_kernel, row_chunk=64),
        grid=(c // tn, m // tm),
        in_specs=[pl.BlockSpec((tm, k), lambda j, i: (i, 0)),
                  pl.BlockSpec((tm, LANES), lambda j, i: (i, 0)),
                  pl.BlockSpec((k, tn), lambda j, i: (0, oa + j)),
                  pl.BlockSpec((k, tn), lambda j, i: (0, og + j)),
                  pl.BlockSpec((None, CONV_WIDTH, tn), lambda j, i: (layer, 0, j)),
                  pl.BlockSpec((None, 1, tn), lambda j, i: (layer, 0, j))],
        out_specs=pl.BlockSpec((tm, tn), lambda j, i: (i, j)),
        out_shape=jax.ShapeDtypeStruct((m, c), F32),
        scratch_shapes=([pltpu.VMEM((CONV_HALO, tn), F32),
                         pltpu.VMEM((MXU_COLS // LANES, SUBLANES - 1, tm + CONV_HALO - SUBLANES,
                                     LANES), F32)]
                        + [pltpu.VMEM((tm + CONV_HALO, LANES), F32) for _ in range(tn // LANES)]),
        compiler_params=_params(("arbitrary", "arbitrary"), 56),
        name="glu_conv",
    )(a, rs, w, w, conv_w, conv_b)


def _conv_norm_kernel(x_ref, lng_ref, lnb_ref, g_ref, out_ref, *, ln_chunk):
    def norm_rows(rc, carry):
        rows = pl.ds(pl.multiple_of(rc * ln_chunk, ln_chunk), ln_chunk)
        x = x_ref[rows, :]
        mu = jnp.mean(x, axis=-1, keepdims=True)
        xc = x - mu
        var = jnp.mean(xc * xc, axis=-1, keepdims=True)
        y = xc * lax.rsqrt(var + LN_EPS) * lng_ref[...] + lnb_ref[...]
        y = y * jax.nn.sigmoid(y)
        ms = jnp.mean(y * y, axis=-1, keepdims=True)
        out_ref[rows, :] = (y * lax.rsqrt(ms + RMS_EPS) * g_ref[...]).astype(out_ref.dtype)
        return carry

    lax.fori_loop(0, x_ref.shape[0] // ln_chunk, norm_rows, 0, unroll=4)


def _conv_norm(x, ln_g, ln_b, out_g, layer, ts=256):
    s_len, c = x.shape
    vec_spec = pl.BlockSpec((None, 1, c), lambda i: (layer, 0, 0))
    return pl.pallas_call(
        functools.partial(_conv_norm_kernel, ln_chunk=16),
        grid=(s_len // ts,),
        in_specs=[pl.BlockSpec((ts, c), lambda i: (i, 0)), vec_spec, vec_spec, vec_spec],
        out_specs=pl.BlockSpec((ts, c), lambda i: (i, 0)),
        out_shape=jax.ShapeDtypeStruct((s_len, c), BF16),
        compiler_params=_params(("parallel",), 32),
        name="conv_norm",
    )(x, ln_g, ln_b, out_g)


def _rotary_tables(seq):
    pos = jnp.arange(seq, dtype=F32)
    inv_freq = ROPE_THETA ** (-(jnp.arange(0, ROT_DIM, 2, dtype=F32) / ROT_DIM))
    ang = pos[:, None] * inv_freq[None, :]
    cos, sin = jnp.cos(ang), jnp.sin(ang)
    half = ROT_DIM // 2
    rest = HEAD_DIM - ROT_DIM
    ones = jnp.ones((seq, rest), F32)
    zeros_h = jnp.zeros((seq, half), F32)
    zeros_r = jnp.zeros((seq, rest), F32)
    cos_t = jnp.concatenate([cos, cos, ones], axis=-1)
    sin_lo = jnp.concatenate([-sin, zeros_h, zeros_r], axis=-1)
    sin_hi = jnp.concatenate([zeros_h, sin, zeros_r], axis=-1)
    return cos_t, sin_lo, sin_hi


def _cast_block(w, n_steps):
    _, rows, cols = w.shape
    for br, bc in ((512, 512), (512, 1024), (1024, 1024)):
        if rows % br == 0 and cols % bc == 0 and (rows // br) * (cols // bc) <= n_steps:
            return br, bc
    raise ValueError("no cast block for %s in %d steps" % (w.shape, n_steps))


def kernel(x, ffn1_norm, ffn1_w_gate, ffn1_w_up, ffn1_w_down, mix_norm, w_in, conv_w, conv_b, conv_ln_g, conv_ln_b, attn_out_norm, conv_out_norm, w_out, ffn2_norm, ffn2_w_gate, ffn2_w_up, ffn2_w_down, final_norm):
    b, s_len, d = x.shape
    assert b == 1 and d == D_MODEL
    x = x.reshape(s_len, d)
    rot = _rotary_tables(s_len)
    a, c = ATTN_WIDTH, CONV_CHANNELS

    def vec(p):
        return p.reshape(p.shape[0], 1, p.shape[1])

    ffn1_norm, mix_norm, ffn2_norm = vec(ffn1_norm), vec(mix_norm), vec(ffn2_norm)
    conv_b, conv_ln_g, conv_ln_b = vec(conv_b), vec(conv_ln_g), vec(conv_ln_b)
    attn_out_norm, conv_out_norm = vec(attn_out_norm), vec(conv_out_norm)
    host_steps = (s_len // 1024) * (D_FF // 512)

    def cast(w, layer):
        return w, layer, _cast_block(w, host_steps)

    gate_b, up_b = ffn1_w_gate[0].astype(BF16), ffn1_w_up[0].astype(BF16)
    xg, rs = _prenorm(x, ffn1_norm, 0)
    for l in range(DEPTH):
        last = l + 1 == DEPTH
        h, (down_b, w_in_b, w_out_b) = _swiglu_mm(
            xg, rs, gate_b, up_b,
            casts=[cast(ffn1_w_down, l), cast(w_in, l), cast(w_out, l)])
        x, xg, rs = _res_mm([h], down_b, x, FFN_RESIDUAL_SCALE, next_norm=(mix_norm, l))
        qkv, (gate_b, up_b) = _qkv_mm(xg, rs, w_in_b, rot,
                                      casts=[cast(ffn2_w_gate, l), cast(ffn2_w_up, l)])
        conv = _glu_conv(xg, rs, w_in_b, 3 * a, 3 * a + c, conv_w, conv_b, l)
        attn_n = _rmsnorm(_attention(qkv), attn_out_norm, l, BF16)
        conv_n = _conv_norm(conv, conv_ln_g, conv_ln_b, conv_out_norm, l)
        x, xg, rs = _res_mm([attn_n, conv_n], w_out_b, x, 1.0, next_norm=(ffn2_norm, l))
        next_casts = [] if last else [cast(ffn1_w_gate, l + 1), cast(ffn1_w_up, l + 1)]
        h, cast_outs = _swiglu_mm(xg, rs, gate_b, up_b,
                                  casts=[cast(ffn2_w_down, l)] + next_casts)
        if last:
            x, = _res_mm([h], cast_outs[0], x, FFN_RESIDUAL_SCALE)
        else:
            gate_b, up_b = cast_outs[1:]
            x, xg, rs = _res_mm([h], cast_outs[0], x, FFN_RESIDUAL_SCALE,
                                next_norm=(ffn1_norm, l + 1))
    out = _rmsnorm(x, final_norm.reshape(1, 1, d), 0, F32)
    return out.reshape(b, s_len, d)
```
